```python
import math
import jax, jax.numpy as jnp
from jax import lax
import numpy as np

D_MODEL = 2048
BATCH = 1
SEQ = 16384
DEPTH = 1
DEC_BATCH = 2
DEC_SEQ = 16384
PAST_LEN = 128

HEAD_DIM = 128
DILATED_GROUPS = ((128, 1), (512, 4), (2048, 16))
HEADS_PER_GROUP_A = 8
N_HEADS_A = HEADS_PER_GROUP_A * len(DILATED_GROUPS)
N_HEADS_B = D_MODEL // HEAD_DIM
N_KV_B = 4
D_FF = ((8 * D_MODEL // 3 + 255) // 256) * 256
GRID_W = 64
ROPE_THETA = 10000.0
Q_BLOCK = 128
EPS = 1e-6
NEG_INF = -1e30

A_W = N_HEADS_A * HEAD_DIM
A_OUT = HEADS_PER_GROUP_A * HEAD_DIM
B_Q = N_HEADS_B * HEAD_DIM
B_KV = N_KV_B * HEAD_DIM
IN_COLS = 3 * A_W + B_Q + 2 * B_KV + 2 * D_MODEL

kernel_name = "hybrid_dilated_gqa_gated_encoder"


def rmsnorm(x, g):
    x32 = x.astype(jnp.float32)
    y = x32 * lax.rsqrt(jnp.mean(x32 * x32, axis=-1, keepdims=True) + EPS)
    return (y * g.astype(jnp.float32)).astype(x.dtype)


def rope(x, pos):
    dim = x.shape[-1]
    half = dim // 2
    inv = ROPE_THETA ** (-(jnp.arange(half, dtype=jnp.float32) * 2.0 / dim))
    ang = pos.astype(jnp.float32)[:, None] * inv[None, :]
    cos = jnp.cos(ang)[:, None, :]
    sin = jnp.sin(ang)[:, None, :]
    x32 = x.astype(jnp.float32)
    x1, x2 = x32[..., :half], x32[..., half:]
    return jnp.concatenate([x1 * cos - x2 * sin, x1 * sin + x2 * cos], axis=-1).astype(x.dtype)


def banded_attention(q, k, v, radius):
    n, L, h, dh = q.shape
    blk = radius
    nb = -(-L // blk)
    lp = nb * blk
    qb = jnp.pad(q, ((0, 0), (0, lp - L), (0, 0), (0, 0))).reshape(n, nb, blk, h, dh)
    kv_pad = ((0, 0), (radius, lp - L + radius), (0, 0), (0, 0))
    kp = jnp.pad(k, kv_pad).reshape(n, nb + 2, blk, h, dh)
    vp = jnp.pad(v, kv_pad).reshape(n, nb + 2, blk, h, dh)

    def windows(a):
        return jnp.concatenate([a[:, :-2], a[:, 1:-1], a[:, 2:]], axis=2)

    kw, vw = windows(kp), windows(vp)
    kpos = jnp.arange((nb + 2) * blk) - radius
    kvalid = ((kpos >= 0) & (kpos < L)).reshape(nb + 2, blk)
    kvalid = jnp.concatenate([kvalid[:-2], kvalid[1:-1], kvalid[2:]], axis=1)
    rel = jnp.arange(3 * blk)[None, :] - blk - jnp.arange(blk)[:, None]
    mask = (jnp.abs(rel) <= radius)[None] & kvalid[:, None, :]
    s = jnp.einsum('nbqhd,nbkhd->nbhqk', qb, kw, preferred_element_type=jnp.float32) * (dh ** -0.5)
    s = jnp.where(mask[None, :, None], s, NEG_INF)
    lse = jax.nn.logsumexp(s, axis=-1)
    p = jnp.exp(s - lse[..., None])
    o = jnp.einsum('nbhqk,nbkhd->nbqhd', p.astype(v.dtype), vw).reshape(n, lp, h, dh)[:, :L]
    lse = lse.transpose(0, 1, 3, 2).reshape(n, lp, h)[:, :L]
    return o, lse


def dilated_group(q, k, v, window, dil):
    b, s, h, dh = q.shape
    L = s // dil

    def to_res(a):
        return a.reshape(b, L, dil, h, dh).transpose(0, 2, 1, 3, 4).reshape(b * dil, L, h, dh)

    o, lse = banded_attention(to_res(q), to_res(k), to_res(v), window // (2 * dil))
    o = o.reshape(b, dil, L, h, dh).transpose(0, 2, 1, 3, 4).reshape(b, s, h, dh)
    lse = lse.reshape(b, dil, L, h).transpose(0, 2, 1, 3).reshape(b, s, h)
    return o, lse


def dense_gqa_blocks(q, k, v):
    b, s, hq, dh = q.shape
    hkv = k.shape[2]
    g = hq // hkv
    nq = s // Q_BLOCK
    qb = q.reshape(b, nq, Q_BLOCK, hkv, g, dh).transpose(1, 0, 2, 3, 4, 5)
    scale = dh ** -0.5

    def one(qblk):
        sc = jnp.einsum('bqhgd,bkhd->bhgqk', qblk, k, preferred_element_type=jnp.float32) * scale
        p = jax.nn.softmax(sc, axis=-1)
        return jnp.einsum('bhgqk,bkhd->bqhgd', p.astype(v.dtype), v)

    o = lax.map(one, qb)
    return o.transpose(1, 0, 2, 3, 4, 5).reshape(b, s, hq * dh)


def encoder_layer(x, g_attn, w_in, q_gain_b, k_gain_b, w_a_br, w_b_br, w_o, g_ffn, w_gate_up, w_down):
    b, s, _ = x.shape
    h = rmsnorm(x, g_attn)
    proj = h @ w_in
    cuts = np.cumsum([A_W, A_W, A_W, B_Q, B_KV, B_KV, D_MODEL]).tolist()
    qa, ka, va, qb, kb, vb, ga, gb = jnp.split(proj, cuts, axis=-1)

    t = jnp.arange(s)
    qa = rope(qa.reshape(b, s, N_HEADS_A, HEAD_DIM), t)
    ka = rope(ka.reshape(b, s, N_HEADS_A, HEAD_DIM), t)
    va = va.reshape(b, s, N_HEADS_A, HEAD_DIM)
    outs, lses = [], []
    for gi, (window, dil) in enumerate(DILATED_GROUPS):
        sl = slice(gi * HEADS_PER_GROUP_A, (gi + 1) * HEADS_PER_GROUP_A)
        o_g, lse_g = dilated_group(qa[:, :, sl], ka[:, :, sl], va[:, :, sl], window, dil)
        outs.append(o_g)
        lses.append(lse_g)
    wts = jax.nn.softmax(jnp.stack(lses, axis=0), axis=0)
    ya = jnp.einsum('gbsh,gbshd->bshd', wts, jnp.stack(outs, axis=0).astype(jnp.float32))
    ya = ya.astype(x.dtype).reshape(b, s, A_OUT)

    n_rows = s // GRID_W
    row = jnp.repeat(jnp.arange(n_rows), GRID_W)
    col = jnp.tile(jnp.arange(GRID_W), n_rows)
    half = HEAD_DIM // 2

    def axial(a):
        return jnp.concatenate([rope(a[..., :half], row), rope(a[..., half:], col)], axis=-1)

    qb = axial(rmsnorm(qb.reshape(b, s, N_HEADS_B, HEAD_DIM), q_gain_b))
    kb = axial(rmsnorm(kb.reshape(b, s, N_KV_B, HEAD_DIM), k_gain_b))
    vb = vb.reshape(b, s, N_KV_B, HEAD_DIM)
    yb = dense_gqa_blocks(qb, kb, vb)

    ya_p = ya @ w_a_br
    yb_p = yb @ w_b_br
    merged = (jax.nn.sigmoid(ga.astype(jnp.float32)) * ya_p.astype(jnp.float32)
              + jax.nn.sigmoid(gb.astype(jnp.float32)) * yb_p.astype(jnp.float32)).astype(x.dtype)
    x = x + merged @ w_o

    h2 = rmsnorm(x, g_ffn)
    gate, up = jnp.split(h2 @ w_gate_up, 2, axis=-1)
    x = x + (jax.nn.silu(gate) * up) @ w_down
    return x


def trunk(x, g_attn, w_in, q_gain_b, k_gain_b, w_a_br, w_b_br, w_o, g_ffn, w_gate_up, w_down, g_final):
    for l in range(DEPTH):
        x = encoder_layer(x, g_attn[l], w_in[l], q_gain_b[l], k_gain_b[l], w_a_br[l], w_b_br[l],
                          w_o[l], g_ffn[l], w_gate_up[l], w_down[l])
    return rmsnorm(x, g_final)


def setup_inputs(seed: int = 0) -> dict:
    key = jax.random.key(seed)
    ks = jax.random.split(key, 14)
    f32 = jnp.float32

    def w(k, shape, fan_in):
        return jax.random.normal(k, shape, f32) * (fan_in ** -0.5)

    def gain(k, shape):
        return 1.0 + 0.02 * jax.random.normal(k, shape, f32)

    return {
        "x_prompt": jax.random.normal(ks[0], (BATCH, SEQ, D_MODEL), f32),
        "x_sample": jax.random.normal(ks[1], (DEC_BATCH, DEC_SEQ, D_MODEL), f32),
        "g_attn": gain(ks[2], (DEPTH, D_MODEL)),
        "w_in": w(ks[3], (DEPTH, D_MODEL, IN_COLS), D_MODEL),
        "q_gain_b": gain(ks[4], (DEPTH, HEAD_DIM)),
        "k_gain_b": gain(ks[5], (DEPTH, HEAD_DIM)),
        "w_a_br": w(ks[6], (DEPTH, A_OUT, D_MODEL), A_OUT),
        "w_b_br": w(ks[7], (DEPTH, B_Q, D_MODEL), B_Q),
        "w_o": w(ks[8], (DEPTH, D_MODEL, D_MODEL), D_MODEL),
        "g_ffn": gain(ks[9], (DEPTH, D_MODEL)),
        "w_gate_up": w(ks[10], (DEPTH, D_MODEL, 2 * D_FF), D_MODEL),
        "w_down": w(ks[11], (DEPTH, D_FF, D_MODEL), D_FF),
        "g_final": gain(ks[12], (D_MODEL,)),
    }


def reference(x_prompt, x_sample, g_attn, w_in, q_gain_b, k_gain_b, w_a_br, w_b_br, w_o, g_ffn,
              w_gate_up, w_down, g_final):
    y_prompt = trunk(x_prompt, g_attn, w_in, q_gain_b, k_gain_b, w_a_br, w_b_br, w_o, g_ffn,
                     w_gate_up, w_down, g_final)
    y_sample = trunk(x_sample, g_attn, w_in, q_gain_b, k_gain_b, w_a_br, w_b_br, w_o, g_ffn,
                     w_gate_up, w_down, g_final)
    return (y_prompt, y_sample)
```

```python
import functools
import math

import numpy as np
import jax
import jax.numpy as jnp
from jax import lax
from jax.experimental import pallas as pl
from jax.experimental.pallas import tpu as pltpu

HEAD_DIM = 128
DILATIONS = (1, 4, 16)
BAND_RADIUS = 64
HEADS_PER_GROUP_A = 8
N_HEADS_A = HEADS_PER_GROUP_A * len(DILATIONS)
N_KV_B = 4
GQA_GROUP = 4
GRID_W = 64
ROPE_THETA = 10000.0
EPS = 1e-6
NEG_BIG = -1e30

LANES = 128
VMEM_BYTES_V7X = 64 * 1024 * 1024
VMEM_LIMIT = VMEM_BYTES_V7X - 8 * 1024 * 1024

F32 = jnp.float32
BF16 = jnp.bfloat16

QSCALE = (HEAD_DIM ** -0.5) * math.log2(math.e)

KV_PAD = BAND_RADIUS * max(DILATIONS)
Q_CHUNK = 2 * KV_PAD
Q_TILE = 256
_HALO = tuple(max(BAND_RADIUS * d, LANES) for d in DILATIONS)
WIN_OFF = tuple(KV_PAD - h for h in _HALO)
WIN_LEN = tuple(Q_TILE + 2 * h for h in _HALO)

DENSE_TQ = 256
DENSE_TK = 512


def _params(sem):
    return pltpu.CompilerParams(dimension_semantics=sem, vmem_limit_bytes=VMEM_LIMIT)


def _rmsnorm_kernel(x_ref, g_ref, o_ref):
    x = x_ref[...]
    ms = jnp.mean(x * x, axis=-1, keepdims=True)
    o_ref[...] = (x * lax.rsqrt(ms + EPS) * g_ref[...]).astype(o_ref.dtype)


def _rmsnorm_bf16(x, g, tm=512):
    b, s, d = x.shape
    return pl.pallas_call(
        _rmsnorm_kernel,
        grid=(b, s // tm),
        in_specs=[pl.BlockSpec((None, tm, d), lambda bi, i: (bi, i, 0)),
                  pl.BlockSpec((1, d), lambda bi, i: (0, 0))],
        out_specs=pl.BlockSpec((None, tm, d), lambda bi, i: (bi, i, 0)),
        out_shape=jax.ShapeDtypeStruct((b, s, d), BF16),
        compiler_params=_params(("parallel", "parallel")),
        name="rmsnorm_bf16",
    )(x, g.reshape(1, d).astype(F32))


def _proj_kernel(*refs, kind, out_kind, nh, rc, tm, pad_tiles, src_tiles):
    h_ref, w_ref = refs[0], refs[1]
    o_ref = refs[-1]
    if kind in ("rope", "qknorm"):
        cos_ref, sin_ref = refs[2], refs[3]
    if kind == "qknorm":
        gain_ref = refs[4]
        lane = lax.broadcasted_iota(jnp.int32, (rc, HEAD_DIM), 1)
        lo = (lane % 64) < 32

    def epilogue(x, c, s):
        if kind == "rope":
            return x * c + pltpu.roll(x, 64, 1) * s
        if kind == "qknorm":
            ms = jnp.mean(x * x, axis=-1, keepdims=True)
            y = x * lax.rsqrt(ms + EPS) * gain_ref[...]
            yr = jnp.where(lo, pltpu.roll(y, 96, 1), pltpu.roll(y, 32, 1))
            return y * c + yr * s
        if kind == "sigmoid":
            return 1.0 / (1.0 + jnp.exp(-x))
        return x

    def compute():
        for r in range(tm // rc):
            r0, r1 = r * rc, (r + 1) * rc
            acc = jnp.dot(h_ref[r0:r1, :], w_ref[...], preferred_element_type=F32)
            c = s = None
            if kind in ("rope", "qknorm"):
                c, s = cos_ref[r0:r1, :], sin_ref[r0:r1, :]
            for j in range(nh):
                y = epilogue(acc[:, j * HEAD_DIM:(j + 1) * HEAD_DIM], c, s)
                if out_kind == "heads":
                    o_ref[j, r0:r1, :] = y.astype(o_ref.dtype)
                elif out_kind == "natural":
                    o_ref[r0:r1, j * HEAD_DIM:(j + 1) * HEAD_DIM] = y.astype(o_ref.dtype)
                else:
                    o_ref[j, r] = y.T.astype(o_ref.dtype)

    if pad_tiles == 0:
        compute()
    else:
        i = pl.program_id(1)
        is_pad = jnp.logical_or(i < pad_tiles, i >= pad_tiles + src_tiles)

        @pl.when(is_pad)
        def _():
            o_ref[...] = jnp.zeros(o_ref.shape, o_ref.dtype)

        @pl.when(jnp.logical_not(is_pad))
        def _():
            compute()


def _proj(h, w, col0, n, *, kind, out_kind, out_dtype=BF16, tables=(), gain=None, pad=0,
          tm=1024, tn=1024, rc=256, name):
    b, s, d = h.shape
    tn = min(tn, n)
    assert col0 % tn == 0 and n % tn == 0
    cb0 = col0 // tn
    nh = tn // HEAD_DIM
    src_tiles = s // tm
    pad_tiles = pad // tm
    rows = s + 2 * pad

    def src(i):
        if pad_tiles == 0:
            return i
        return jnp.clip(i - pad_tiles, 0, src_tiles - 1)

    in_specs = [pl.BlockSpec((None, tm, d), lambda bi, i, j: (bi, src(i), 0)),
                pl.BlockSpec((d, tn), lambda bi, i, j: (0, cb0 + j))]
    args = [h, w]
    for t in tables:
        in_specs.append(pl.BlockSpec((tm, HEAD_DIM), lambda bi, i, j: (src(i), 0)))
        args.append(t)
    if gain is not None:
        in_specs.append(pl.BlockSpec((1, HEAD_DIM), lambda bi, i, j: (0, 0)))
        args.append(gain)

    if out_kind == "heads":
        out_shape = (b, n // HEAD_DIM, rows, HEAD_DIM)
        out_spec = pl.BlockSpec((None, nh, tm, HEAD_DIM), lambda bi, i, j: (bi, j, i, 0))
    elif out_kind == "natural":
        out_shape = (b, rows, n)
        out_spec = pl.BlockSpec((None, tm, tn), lambda bi, i, j: (bi, i, j))
    else:
        out_shape = (b, n // HEAD_DIM, rows // rc, HEAD_DIM, rc)
        out_spec = pl.BlockSpec((None, nh, tm // rc, HEAD_DIM, rc),
                                lambda bi, i, j: (bi, j, i, 0, 0))

    kern = functools.partial(_proj_kernel, kind=kind, out_kind=out_kind, nh=nh, rc=rc, tm=tm,
                             pad_tiles=pad_tiles, src_tiles=src_tiles)
    return pl.pallas_call(
        kern,
        grid=(b, rows // tm, n // tn),
        in_specs=in_specs,
        out_specs=out_spec,
        out_shape=jax.ShapeDtypeStruct(out_shape, out_dtype),
        compiler_params=_params(("parallel", "parallel", "arbitrary")),
        name=name,
    )(*args)


def _band_bias(g):
    d = DILATIONS[g]
    qi = np.arange(Q_TILE)[:, None]
    kj = np.arange(WIN_LEN[g])[None, :]
    rel = kj + WIN_OFF[g] - KV_PAD - qi
    ok = (rel % d == 0) & (np.abs(rel) <= BAND_RADIUS * d)
    return np.where(ok, 0.0, NEG_BIG).astype(np.float32)


def _banded_kernel(*refs, seq):
    ng = len(DILATIONS)
    q_refs = refs[0:ng]
    k_refs = refs[ng:3 * ng]
    v_refs = refs[3 * ng:5 * ng]
    bias_refs = refs[5 * ng:6 * ng]
    o_ref = refs[6 * ng]
    kbuf, vbuf = refs[6 * ng + 1], refs[6 * ng + 2]
    c = pl.program_id(2)

    for g in range(ng):
        kbuf[g, 0:Q_CHUNK, :] = k_refs[2 * g][...]
        kbuf[g, Q_CHUNK:2 * Q_CHUNK, :] = k_refs[2 * g + 1][...]
        vbuf[g, 0:Q_CHUNK, :] = v_refs[2 * g][...]
        vbuf[g, Q_CHUNK:2 * Q_CHUNK, :] = v_refs[2 * g + 1][...]

    def tile(t, carry):
        r0 = pl.multiple_of(t * Q_TILE, Q_TILE)
        m = jnp.full((Q_TILE, 1), NEG_BIG, F32)
        l = jnp.zeros((Q_TILE, 1), F32)
        acc = jnp.zeros((Q_TILE, HEAD_DIM), F32)
        for g in range(ng):
            q = q_refs[g][pl.ds(r0, Q_TILE), :]
            off = pl.multiple_of(r0 + WIN_OFF[g], LANES)
            kw = kbuf[g, pl.ds(off, WIN_LEN[g]), :]
            vw = vbuf[g, pl.ds(off, WIN_LEN[g]), :]
            s = lax.dot_general(q, kw, (((1,), (1,)), ((), ())), preferred_element_type=F32)
            kpos = c * Q_CHUNK + off + lax.broadcasted_iota(jnp.int32, (1, WIN_LEN[g]), 1)
            kvalid = jnp.where((kpos >= KV_PAD) & (kpos < KV_PAD + seq), 0.0, NEG_BIG)
            s = s + bias_refs[g][...] + kvalid
            m_new = jnp.maximum(m, jnp.max(s, axis=-1, keepdims=True))
            alpha = jnp.exp2(m - m_new)
            p = jnp.exp2(s - m_new)
            l = alpha * l + jnp.sum(p, axis=-1, keepdims=True)
            acc = alpha * acc + jnp.dot(p.astype(BF16), vw, preferred_element_type=F32)
            m = m_new
        o_ref[pl.ds(r0, Q_TILE), :] = (acc / l).astype(o_ref.dtype)
        return carry

    lax.fori_loop(0, Q_CHUNK // Q_TILE, tile, 0)


def _banded_attention(qa, ka, va):
    b, _, s, _ = qa.shape
    ng = len(DILATIONS)
    hpg = HEADS_PER_GROUP_A
    blk = (None, None, Q_CHUNK, HEAD_DIM)
    in_specs, args = [], []
    for g in range(ng):
        in_specs.append(pl.BlockSpec(blk, lambda bi, hg, c, g=g: (bi, g * hpg + hg, c, 0)))
        args.append(qa)
    for arr in (ka, va):
        for g in range(ng):
            in_specs.append(pl.BlockSpec(blk, lambda bi, hg, c, g=g: (bi, g * hpg + hg, c, 0)))
            in_specs.append(pl.BlockSpec(blk, lambda bi, hg, c, g=g: (bi, g * hpg + hg, c + 1, 0)))
            args += [arr, arr]
    for g in range(ng):
        in_specs.append(pl.BlockSpec((Q_TILE, WIN_LEN[g]), lambda bi, hg, c: (0, 0)))
        args.append(jnp.asarray(_band_bias(g)))
    return pl.pallas_call(
        functools.partial(_banded_kernel, seq=s),
        grid=(b, hpg, s // Q_CHUNK),
        in_specs=in_specs,
        out_specs=pl.BlockSpec((None, Q_CHUNK, HEAD_DIM), lambda bi, hg, c: (bi, c, hg)),
        out_shape=jax.ShapeDtypeStruct((b, s, hpg * HEAD_DIM), BF16),
        scratch_shapes=[pltpu.VMEM((ng, 2 * Q_CHUNK, HEAD_DIM), BF16),
                        pltpu.VMEM((ng, 2 * Q_CHUNK, HEAD_DIM), BF16)],
        compiler_params=_params(("parallel", "parallel", "arbitrary")),
        name="banded_attention",
    )(*args)


def _dense_kernel(q_ref, k_ref, vt_ref, o_ref, acc_ref, *, tq, tk):
    nq = GQA_GROUP * tq
    q = jnp.concatenate([q_ref[:, g * HEAD_DIM:(g + 1) * HEAD_DIM] for g in range(GQA_GROUP)],
                        axis=0)
    acc_ref[...] = jnp.zeros(acc_ref.shape, F32)

    def step(c, carry):
        m, l = carry
        k0 = pl.multiple_of(c * tk, tk)
        st = lax.dot_general(k_ref[pl.ds(k0, tk), :], q, (((1,), (1,)), ((), ())),
                             preferred_element_type=F32)
        m_new = jnp.maximum(m, jnp.max(st, axis=0, keepdims=True))
        alpha = jnp.exp2(m - m_new)
        p = jnp.exp2(st - m_new)
        l = alpha * l + jnp.sum(p, axis=0, keepdims=True)
        pv = jnp.dot(vt_ref[c], p.astype(BF16), preferred_element_type=F32)
        acc_ref[...] = acc_ref[...] * alpha + pv
        return m_new, l

    m0 = jnp.full((1, nq), NEG_BIG, F32)
    l0 = jnp.zeros((1, nq), F32)
    _, l = lax.fori_loop(0, k_ref.shape[0] // tk, step, (m0, l0))
    o = acc_ref[...] / l
    for g in range(GQA_GROUP):
        o_ref[:, g * HEAD_DIM:(g + 1) * HEAD_DIM] = o[:, g * tq:(g + 1) * tq].T.astype(o_ref.dtype)


def _dense_attention(qb, kb, vbt, tq=DENSE_TQ):
    b, s, dq = qb.shape
    tk = vbt.shape[-1]
    gw = GQA_GROUP * HEAD_DIM
    return pl.pallas_call(
        functools.partial(_dense_kernel, tq=tq, tk=tk),
        grid=(b, N_KV_B, s // tq),
        in_specs=[pl.BlockSpec((None, tq, gw), lambda bi, h, i: (bi, i, h)),
                  pl.BlockSpec((None, None, s, HEAD_DIM), lambda bi, h, i: (bi, h, 0, 0)),
                  pl.BlockSpec((None, None, s // tk, HEAD_DIM, tk),
                               lambda bi, h, i: (bi, h, 0, 0, 0))],
        out_specs=pl.BlockSpec((None, tq, gw), lambda bi, h, i: (bi, i, h)),
        out_shape=jax.ShapeDtypeStruct((b, s, dq), BF16),
        scratch_shapes=[pltpu.VMEM((HEAD_DIM, GQA_GROUP * tq), F32)],
        compiler_params=_params(("parallel", "parallel", "arbitrary")),
        name="dense_gqa_attention",
    )(qb, kb, vbt)


def _merge_kernel(ya_ref, yb_ref, wa_ref, wb_ref, sa_ref, sb_ref, o_ref, *, tm, rc):
    for r in range(tm // rc):
        r0, r1 = r * rc, (r + 1) * rc
        pa = jnp.dot(ya_ref[r0:r1, :], wa_ref[...], preferred_element_type=F32)
        pb = jnp.dot(yb_ref[r0:r1, :], wb_ref[...], preferred_element_type=F32)
        o_ref[r0:r1, :] = (sa_ref[r0:r1, :] * pa + sb_ref[r0:r1, :] * pb).astype(o_ref.dtype)


def _gated_merge(ya, yb, wa, wb, gates, tm=512, tn=1024, rc=256):
    b, s, da = ya.shape
    db = yb.shape[-1]
    d = wa.shape[1]
    nt = d // tn
    return pl.pallas_call(
        functools.partial(_merge_kernel, tm=tm, rc=rc),
        grid=(b, s // tm, nt),
        in_specs=[pl.BlockSpec((None, tm, da), lambda bi, i, j: (bi, i, 0)),
                  pl.BlockSpec((None, tm, db), lambda bi, i, j: (bi, i, 0)),
                  pl.BlockSpec((da, tn), lambda bi, i, j: (0, j)),
                  pl.BlockSpec((db, tn), lambda bi, i, j: (0, j)),
                  pl.BlockSpec((None, tm, tn), lambda bi, i, j: (bi, i, j)),
                  pl.BlockSpec((None, tm, tn), lambda bi, i, j: (bi, i, nt + j))],
        out_specs=pl.BlockSpec((None, tm, tn), lambda bi, i, j: (bi, i, j)),
        out_shape=jax.ShapeDtypeStruct((b, s, d), BF16),
        compiler_params=_params(("parallel", "parallel", "arbitrary")),
        name="gated_merge",
    )(ya, yb, wa, wb, gates, gates)


def _oproj_kernel(m_ref, w_ref, x_ref, o_ref, *, tm, rc):
    for r in range(tm // rc):
        r0, r1 = r * rc, (r + 1) * rc
        o_ref[r0:r1, :] = x_ref[r0:r1, :] + jnp.dot(m_ref[r0:r1, :], w_ref[...],
                                                    preferred_element_type=F32)


def _oproj_residual(merged, w_o, x, tm=1024, tn=1024, rc=256):
    b, s, d = merged.shape
    return pl.pallas_call(
        functools.partial(_oproj_kernel, tm=tm, rc=rc),
        grid=(b, s // tm, d // tn),
        in_specs=[pl.BlockSpec((None, tm, d), lambda bi, i, j: (bi, i, 0)),
                  pl.BlockSpec((d, tn), lambda bi, i, j: (0, j)),
                  pl.BlockSpec((None, tm, tn), lambda bi, i, j: (bi, i, j))],
        out_specs=pl.BlockSpec((None, tm, tn), lambda bi, i, j: (bi, i, j)),
        out_shape=jax.ShapeDtypeStruct((b, s, d), F32),
        compiler_params=_params(("parallel", "parallel", "arbitrary")),
        name="oproj_residual",
    )(merged, w_o, x)


def _ffn_kernel(x_ref, gffn_ref, gfin_ref, wg_ref, wu_ref, wd_ref, o_ref, h_ref, acc_ref):
    f = pl.program_id(2)

    @pl.when(f == 0)
    def _():
        x = x_ref[...]
        ms = jnp.mean(x * x, axis=-1, keepdims=True)
        h_ref[...] = (x * lax.rsqrt(ms + EPS) * gffn_ref[...]).astype(h_ref.dtype)
        acc_ref[...] = jnp.zeros(acc_ref.shape, F32)

    h = h_ref[...]
    gate = jnp.dot(h, wg_ref[...], preferred_element_type=F32)
    up = jnp.dot(h, wu_ref[...], preferred_element_type=F32)
    act = (gate / (1.0 + jnp.exp(-gate))) * up
    acc_ref[...] += jnp.dot(act.astype(BF16), wd_ref[...], preferred_element_type=F32)

    @pl.when(f == pl.num_programs(2) - 1)
    def _():
        y = x_ref[...] + acc_ref[...]
        ms = jnp.mean(y * y, axis=-1, keepdims=True)
        o_ref[...] = y * lax.rsqrt(ms + EPS) * gfin_ref[...]


def _ffn_final(x, g_ffn, g_final, w_gu, w_down, tm=512, tf=512):
    b, s, d = x.shape
    dff = w_down.shape[0]
    nf = dff // tf
    return pl.pallas_call(
        _ffn_kernel,
        grid=(b, s // tm, nf),
        in_specs=[pl.BlockSpec((None, tm, d), lambda bi, i, f: (bi, i, 0)),
                  pl.BlockSpec((1, d), lambda bi, i, f: (0, 0)),
                  pl.BlockSpec((1, d), lambda bi, i, f: (0, 0)),
                  pl.BlockSpec((d, tf), lambda bi, i, f: (0, f)),
                  pl.BlockSpec((d, tf), lambda bi, i, f: (0, nf + f)),
                  pl.BlockSpec((tf, d), lambda bi, i, f: (f, 0))],
        out_specs=pl.BlockSpec((None, tm, d), lambda bi, i, f: (bi, i, 0)),
        out_shape=jax.ShapeDtypeStruct((b, s, d), F32),
        scratch_shapes=[pltpu.VMEM((tm, d), BF16), pltpu.VMEM((tm, d), F32)],
        compiler_params=_params(("parallel", "parallel", "arbitrary")),
        name="ffn_final",
    )(x, g_ffn.reshape(1, d).astype(F32), g_final.reshape(1, d).astype(F32), w_gu, w_gu, w_down)


def _rope_tables(s):
    half = HEAD_DIM // 2
    inv = ROPE_THETA ** (-(jnp.arange(half, dtype=F32) * 2.0 / HEAD_DIM))
    ang = jnp.arange(s).astype(F32)[:, None] * inv[None, :]
    cos, sin = jnp.cos(ang), jnp.sin(ang)
    return jnp.concatenate([cos, cos], axis=-1), jnp.concatenate([-sin, sin], axis=-1)


def _axial_tables(s):
    dim = HEAD_DIM // 2
    half = dim // 2
    inv = ROPE_THETA ** (-(jnp.arange(half, dtype=F32) * 2.0 / dim))
    n_rows = s // GRID_W
    row = jnp.repeat(jnp.arange(n_rows), GRID_W).astype(F32)
    col = jnp.tile(jnp.arange(GRID_W), n_rows).astype(F32)
    cs, sn = [], []
    for pos in (row, col):
        ang = pos[:, None] * inv[None, :]
        c, sgn = jnp.cos(ang), jnp.sin(ang)
        cs += [c, c]
        sn += [-sgn, sgn]
    return jnp.concatenate(cs, axis=-1), jnp.concatenate(sn, axis=-1)


def _trunk(x, p):
    b, s, d = x.shape
    a_w = N_HEADS_A * HEAD_DIM
    b_q = d
    b_kv = N_KV_B * HEAD_DIM
    cuts = np.cumsum([0, a_w, a_w, a_w, b_q, b_kv, b_kv, 2 * d]).tolist()
    w_in = p["w_in"]

    def wcols(k):
        return w_in, cuts[k], cuts[k + 1] - cuts[k]

    h = _rmsnorm_bf16(x, p["g_attn"])
    cos1, sin1 = p["rope"]
    cos2, sin2 = p["axial"]

    qa = _proj(h, *wcols(0), kind="rope", out_kind="heads", tables=(cos1 * QSCALE, sin1 * QSCALE),
               name="proj_qa")
    ka = _proj(h, *wcols(1), kind="rope", out_kind="heads", tables=(cos1, sin1), pad=KV_PAD,
               name="proj_ka")
    va = _proj(h, *wcols(2), kind="plain", out_kind="heads", pad=KV_PAD, name="proj_va")
    qb = _proj(h, *wcols(3), kind="qknorm", out_kind="natural", tables=(cos2, sin2),
               gain=p["q_gain"] * QSCALE, name="proj_qb")
    kb = _proj(h, *wcols(4), kind="qknorm", out_kind="heads", tables=(cos2, sin2),
               gain=p["k_gain"], name="proj_kb")
    vbt = _proj(h, *wcols(5), kind="plain", out_kind="vT", rc=DENSE_TK, name="proj_vb")
    gates = _proj(h, *wcols(6), kind="sigmoid", out_kind="natural", out_dtype=F32, name="proj_gates")

    ya = _banded_attention(qa, ka, va)
    yb = _dense_attention(qb, kb, vbt)
    merged = _gated_merge(ya, yb, p["w_a_br"], p["w_b_br"], gates)
    x1 = _oproj_residual(merged, p["w_o"], x)
    return _ffn_final(x1, p["g_ffn"], p["g_final"], p["w_gate_up"], p["w_down"])


def kernel(x_prompt, x_sample, g_attn, w_in, q_gain_b, k_gain_b, w_a_br, w_b_br, w_o, g_ffn,
           w_gate_up, w_down, g_final):
    assert g_attn.shape[0] == 1, "single-layer trunk"
    s = x_prompt.shape[1]
    assert x_sample.shape[1] == s and s % Q_CHUNK == 0
    p = {
        "g_attn": g_attn[0], "g_ffn": g_ffn[0], "g_final": g_final,
        "w_in": w_in[0].astype(BF16),
        "q_gain": q_gain_b[0].reshape(1, HEAD_DIM).astype(F32),
        "k_gain": k_gain_b[0].reshape(1, HEAD_DIM).astype(F32),
        "w_a_br": w_a_br[0].astype(BF16), "w_b_br": w_b_br[0].astype(BF16),
        "w_o": w_o[0].astype(BF16),
        "w_gate_up": w_gate_up[0].astype(BF16), "w_down": w_down[0].astype(BF16),
        "rope": _rope_tables(s), "axial": _axial_tables(s),
    }
    return (_trunk(x_prompt, p), _trunk(x_sample, p))
```

```python
import functools
import math

import numpy as np
import jax
import jax.numpy as jnp
from jax import lax
from jax.experimental import pallas as pl
from jax.experimental.pallas import tpu as pltpu

HEAD_DIM = 128
DILATIONS = (1, 4, 16)
BAND_RADIUS = 64
HEADS_PER_GROUP_A = 8
N_HEADS_A = HEADS_PER_GROUP_A * len(DILATIONS)
N_KV_B = 4
GQA_GROUP = 4
GRID_W = 64
ROPE_THETA = 10000.0
EPS = 1e-6
NEG_BIG = -1e30

LANES = 128
VMEM_BYTES_V7X = 64 * 1024 * 1024
VMEM_LIMIT = VMEM_BYTES_V7X - 8 * 1024 * 1024

F32 = jnp.float32
BF16 = jnp.bfloat16

QSCALE = (HEAD_DIM ** -0.5) * math.log2(math.e)

KV_PAD = BAND_RADIUS * max(DILATIONS)
Q_CHUNK = 2 * KV_PAD
Q_TILE = 256
_HALO = tuple(max(BAND_RADIUS * d, LANES) for d in DILATIONS)
WIN_OFF = tuple(KV_PAD - h for h in _HALO)
WIN_LEN = tuple(Q_TILE + 2 * h for h in _HALO)

DENSE_TQ = 256
DENSE_TK = 512


def _params(sem):
    return pltpu.CompilerParams(dimension_semantics=sem, vmem_limit_bytes=VMEM_LIMIT)


def _rmsnorm_kernel(x_ref, g_ref, o_ref):
    x = x_ref[...]
    ms = jnp.mean(x * x, axis=-1, keepdims=True)
    o_ref[...] = (x * lax.rsqrt(ms + EPS) * g_ref[...]).astype(o_ref.dtype)


def _rmsnorm_bf16(x, g, tm=512):
    b, s, d = x.shape
    return pl.pallas_call(
        _rmsnorm_kernel,
        grid=(b, s // tm),
        in_specs=[pl.BlockSpec((None, tm, d), lambda bi, i: (bi, i, 0)),
                  pl.BlockSpec((1, d), lambda bi, i: (0, 0))],
        out_specs=pl.BlockSpec((None, tm, d), lambda bi, i: (bi, i, 0)),
        out_shape=jax.ShapeDtypeStruct((b, s, d), BF16),
        compiler_params=_params(("parallel", "parallel")),
        name="rmsnorm_bf16",
    )(x, g.reshape(1, d).astype(F32))


def _proj_kernel(*refs, kind, out_kind, nh, rc, tm, pad_tiles, src_tiles):
    h_ref, w_ref = refs[0], refs[1]
    o_ref = refs[-1]
    if kind in ("rope", "qknorm"):
        cos_ref, sin_ref = refs[2], refs[3]
    if kind == "qknorm":
        gain_ref = refs[4]
        lane = lax.broadcasted_iota(jnp.int32, (rc, HEAD_DIM), 1)
        lo = (lane % 64) < 32

    def epilogue(x, c, s):
        if kind == "rope":
            return x * c + pltpu.roll(x, 64, 1) * s
        if kind == "qknorm":
            ms = jnp.mean(x * x, axis=-1, keepdims=True)
            y = x * lax.rsqrt(ms + EPS) * gain_ref[...]
            yr = jnp.where(lo, pltpu.roll(y, 96, 1), pltpu.roll(y, 32, 1))
            return y * c + yr * s
        if kind == "sigmoid":
            return 1.0 / (1.0 + jnp.exp(-x))
        return x

    def compute():
        for r in range(tm // rc):
            r0, r1 = r * rc, (r + 1) * rc
            acc = jnp.dot(h_ref[r0:r1, :], w_ref[...], preferred_element_type=F32)
            c = s = None
            if kind in ("rope", "qknorm"):
                c, s = cos_ref[r0:r1, :], sin_ref[r0:r1, :]
            for j in range(nh):
                y = epilogue(acc[:, j * HEAD_DIM:(j + 1) * HEAD_DIM], c, s)
                if out_kind == "heads":
                    o_ref[j, r0:r1, :] = y.astype(o_ref.dtype)
                elif out_kind == "natural":
                    o_ref[r0:r1, j * HEAD_DIM:(j + 1) * HEAD_DIM] = y.astype(o_ref.dtype)
                else:
                    o_ref[j, r] = y.T.astype(o_ref.dtype)

    if pad_tiles == 0:
        compute()
    else:
        i = pl.program_id(1)
        is_pad = jnp.logical_or(i < pad_tiles, i >= pad_tiles + src_tiles)

        @pl.when(is_pad)
        def _():
            o_ref[...] = jnp.zeros(o_ref.shape, o_ref.dtype)

        @pl.when(jnp.logical_not(is_pad))
        def _():
            compute()


def _proj(h, w, col0, n, *, kind, out_kind, out_dtype=BF16, tables=(), gain=None, pad=0,
          tm=1024, tn=1024, rc=256, name):
    b, s, d = h.shape
    tn = min(tn, n)
    assert col0 % tn == 0 and n % tn == 0
    cb0 = col0 // tn
    nh = tn // HEAD_DIM
    src_tiles = s // tm
    pad_tiles = pad // tm
    rows = s + 2 * pad

    def src(i):
        if pad_tiles == 0:
            return i
        return jnp.clip(i - pad_tiles, 0, src_tiles - 1)

    in_specs = [pl.BlockSpec((None, tm, d), lambda bi, i, j: (bi, src(i), 0)),
                pl.BlockSpec((d, tn), lambda bi, i, j: (0, cb0 + j))]
    args = [h, w]
    for t in tables:
        in_specs.append(pl.BlockSpec((tm, HEAD_DIM), lambda bi, i, j: (src(i), 0)))
        args.append(t)
    if gain is not None:
        in_specs.append(pl.BlockSpec((1, HEAD_DIM), lambda bi, i, j: (0, 0)))
        args.append(gain)

    if out_kind == "heads":
        out_shape = (b, n // HEAD_DIM, rows, HEAD_DIM)
        out_spec = pl.BlockSpec((None, nh, tm, HEAD_DIM), lambda bi, i, j: (bi, j, i, 0))
    elif out_kind == "natural":
        out_shape = (b, rows, n)
        out_spec = pl.BlockSpec((None, tm, tn), lambda bi, i, j: (bi, i, j))
    else:
        out_shape = (b, n // HEAD_DIM, rows // rc, HEAD_DIM, rc)
        out_spec = pl.BlockSpec((None, nh, tm // rc, HEAD_DIM, rc),
                                lambda bi, i, j: (bi, j, i, 0, 0))

    kern = functools.partial(_proj_kernel, kind=kind, out_kind=out_kind, nh=nh, rc=rc, tm=tm,
                             pad_tiles=pad_tiles, src_tiles=src_tiles)
    return pl.pallas_call(
        kern,
        grid=(b, rows // tm, n // tn),
        in_specs=in_specs,
        out_specs=out_spec,
        out_shape=jax.ShapeDtypeStruct(out_shape, out_dtype),
        compiler_params=_params(("parallel", "parallel", "arbitrary")),
        name=name,
    )(*args)


def _band_bias(g):
    d = DILATIONS[g]
    qi = np.arange(Q_TILE)[:, None]
    kj = np.arange(WIN_LEN[g])[None, :]
    rel = kj + WIN_OFF[g] - KV_PAD - qi
    ok = (rel % d == 0) & (np.abs(rel) <= BAND_RADIUS * d)
    return np.where(ok, 0.0, NEG_BIG).astype(np.float32)


def _banded_kernel(*refs, seq):
    ng = len(DILATIONS)
    q_refs = refs[0:ng]
    k_refs = refs[ng:3 * ng]
    v_refs = refs[3 * ng:5 * ng]
    bias_refs = refs[5 * ng:6 * ng]
    o_ref = refs[6 * ng]
    kbuf, vbuf = refs[6 * ng + 1], refs[6 * ng + 2]
    c = pl.program_id(2)

    for g in range(ng):
        kbuf[g, 0:Q_CHUNK, :] = k_refs[2 * g][...]
        kbuf[g, Q_CHUNK:2 * Q_CHUNK, :] = k_refs[2 * g + 1][...]
        vbuf[g, 0:Q_CHUNK, :] = v_refs[2 * g][...]
        vbuf[g, Q_CHUNK:2 * Q_CHUNK, :] = v_refs[2 * g + 1][...]

    def tile(t, carry):
        r0 = pl.multiple_of(t * Q_TILE, Q_TILE)
        m = jnp.full((Q_TILE, 1), NEG_BIG, F32)
        l = jnp.zeros((Q_TILE, 1), F32)
        acc = jnp.zeros((Q_TILE, HEAD_DIM), F32)
        for g in range(ng):
            q = q_refs[g][pl.ds(r0, Q_TILE), :]
            off = pl.multiple_of(r0 + WIN_OFF[g], LANES)
            kw = kbuf[g, pl.ds(off, WIN_LEN[g]), :]
            vw = vbuf[g, pl.ds(off, WIN_LEN[g]), :]
            s = lax.dot_general(q, kw, (((1,), (1,)), ((), ())), preferred_element_type=F32)
            kpos = c * Q_CHUNK + off + lax.broadcasted_iota(jnp.int32, (1, WIN_LEN[g]), 1)
            kvalid = jnp.where((kpos >= KV_PAD) & (kpos < KV_PAD + seq), 0.0, NEG_BIG)
            s = s + bias_refs[g][...] + kvalid
            m_new = jnp.maximum(m, jnp.max(s, axis=-1, keepdims=True))
            alpha = jnp.exp2(m - m_new)
            p = jnp.exp2(s - m_new)
            l = alpha * l + jnp.sum(p, axis=-1, keepdims=True)
            acc = alpha * acc + jnp.dot(p.astype(BF16), vw, preferred_element_type=F32)
            m = m_new
        o_ref[pl.ds(r0, Q_TILE), :] = (acc / l).astype(o_ref.dtype)
        return carry

    lax.fori_loop(0, Q_CHUNK // Q_TILE, tile, 0)


def _banded_attention(qa, ka, va):
    b, _, s, _ = qa.shape
    ng = len(DILATIONS)
    hpg = HEADS_PER_GROUP_A
    blk = (None, None, Q_CHUNK, HEAD_DIM)
    in_specs, args = [], []
    for g in range(ng):
        in_specs.append(pl.BlockSpec(blk, lambda bi, hg, c, g=g: (bi, g * hpg + hg, c, 0)))
        args.append(qa)
    for arr in (ka, va):
        for g in range(ng):
            in_specs.append(pl.BlockSpec(blk, lambda bi, hg, c, g=g: (bi, g * hpg + hg, c, 0)))
            in_specs.append(pl.BlockSpec(blk, lambda bi, hg, c, g=g: (bi, g * hpg + hg, c + 1, 0)))
            args += [arr, arr]
    for g in range(ng):
        in_specs.append(pl.BlockSpec((Q_TILE, WIN_LEN[g]), lambda bi, hg, c: (0, 0)))
        args.append(jnp.asarray(_band_bias(g)))
    return pl.pallas_call(
        functools.partial(_banded_kernel, seq=s),
        grid=(b, hpg, s // Q_CHUNK),
        in_specs=in_specs,
        out_specs=pl.BlockSpec((None, Q_CHUNK, HEAD_DIM), lambda bi, hg, c: (bi, c, hg)),
        out_shape=jax.ShapeDtypeStruct((b, s, hpg * HEAD_DIM), BF16),
        scratch_shapes=[pltpu.VMEM((ng, 2 * Q_CHUNK, HEAD_DIM), BF16),
                        pltpu.VMEM((ng, 2 * Q_CHUNK, HEAD_DIM), BF16)],
        compiler_params=_params(("parallel", "parallel", "arbitrary")),
        name="banded_attention",
    )(*args)


def _dense_kernel(q_ref, k_ref, vt_ref, o_ref, acc_ref, s_ref, *, tq, tk):
    nq = GQA_GROUP * tq
    n_chunks = k_ref.shape[0] // tk
    q = jnp.concatenate([q_ref[:, g * HEAD_DIM:(g + 1) * HEAD_DIM] for g in range(GQA_GROUP)],
                        axis=0)
    acc_ref[...] = jnp.zeros(acc_ref.shape, F32)

    def scores(c):
        k0 = pl.multiple_of(c * tk, tk)
        return lax.dot_general(k_ref[pl.ds(k0, tk), :], q, (((1,), (1,)), ((), ())),
                               preferred_element_type=F32)

    def update(c, slot, m, l):
        st = s_ref[slot]
        m_new = jnp.maximum(m, jnp.max(st, axis=0, keepdims=True))
        alpha = jnp.exp2(m - m_new)
        p = jnp.exp2(st - m_new)
        l = alpha * l + jnp.sum(p, axis=0, keepdims=True)
        pv = jnp.dot(vt_ref[c], p.astype(BF16), preferred_element_type=F32)
        acc_ref[...] = acc_ref[...] * alpha + pv
        return m_new, l

    s_ref[0] = scores(0)

    def pair(j, carry):
        m, l = carry
        c0 = 2 * j
        s_ref[1] = scores(c0 + 1)
        m, l = update(c0, 0, m, l)
        s_ref[0] = scores(jnp.minimum(c0 + 2, n_chunks - 1))
        return update(c0 + 1, 1, m, l)

    m0 = jnp.full((1, nq), NEG_BIG, F32)
    l0 = jnp.zeros((1, nq), F32)
    _, l = lax.fori_loop(0, n_chunks // 2, pair, (m0, l0))
    o = acc_ref[...] / l
    for g in range(GQA_GROUP):
        o_ref[:, g * HEAD_DIM:(g + 1) * HEAD_DIM] = o[:, g * tq:(g + 1) * tq].T.astype(o_ref.dtype)


def _dense_attention(qb, kb, vbt, tq=DENSE_TQ):
    b, s, dq = qb.shape
    tk = vbt.shape[-1]
    gw = GQA_GROUP * HEAD_DIM
    return pl.pallas_call(
        functools.partial(_dense_kernel, tq=tq, tk=tk),
        grid=(b, N_KV_B, s // tq),
        in_specs=[pl.BlockSpec((None, tq, gw), lambda bi, h, i: (bi, i, h)),
                  pl.BlockSpec((None, None, s, HEAD_DIM), lambda bi, h, i: (bi, h, 0, 0)),
                  pl.BlockSpec((None, None, s // tk, HEAD_DIM, tk),
                               lambda bi, h, i: (bi, h, 0, 0, 0))],
        out_specs=pl.BlockSpec((None, tq, gw), lambda bi, h, i: (bi, i, h)),
        out_shape=jax.ShapeDtypeStruct((b, s, dq), BF16),
        scratch_shapes=[pltpu.VMEM((HEAD_DIM, GQA_GROUP * tq), F32),
                        pltpu.VMEM((2, tk, GQA_GROUP * tq), F32)],
        compiler_params=_params(("parallel", "parallel", "arbitrary")),
        name="dense_gqa_attention",
    )(qb, kb, vbt)


def _merge_kernel(ya_ref, yb_ref, wa_ref, wb_ref, sa_ref, sb_ref, o_ref, *, tm, rc):
    for r in range(tm // rc):
        r0, r1 = r * rc, (r + 1) * rc
        pa = jnp.dot(ya_ref[r0:r1, :], wa_ref[...], preferred_element_type=F32)
        pb = jnp.dot(yb_ref[r0:r1, :], wb_ref[...], preferred_element_type=F32)
        o_ref[r0:r1, :] = (sa_ref[r0:r1, :] * pa + sb_ref[r0:r1, :] * pb).astype(o_ref.dtype)


def _gated_merge(ya, yb, wa, wb, gates, tm=512, tn=1024, rc=256):
    b, s, da = ya.shape
    db = yb.shape[-1]
    d = wa.shape[1]
    nt = d // tn
    return pl.pallas_call(
        functools.partial(_merge_kernel, tm=tm, rc=rc),
        grid=(b, s // tm, nt),
        in_specs=[pl.BlockSpec((None, tm, da), lambda bi, i, j: (bi, i, 0)),
                  pl.BlockSpec((None, tm, db), lambda bi, i, j: (bi, i, 0)),
                  pl.BlockSpec((da, tn), lambda bi, i, j: (0, j)),
                  pl.BlockSpec((db, tn), lambda bi, i, j: (0, j)),
                  pl.BlockSpec((None, tm, tn), lambda bi, i, j: (bi, i, j)),
                  pl.BlockSpec((None, tm, tn), lambda bi, i, j: (bi, i, nt + j))],
        out_specs=pl.BlockSpec((None, tm, tn), lambda bi, i, j: (bi, i, j)),
        out_shape=jax.ShapeDtypeStruct((b, s, d), BF16),
        compiler_params=_params(("parallel", "parallel", "arbitrary")),
        name="gated_merge",
    )(ya, yb, wa, wb, gates, gates)


def _oproj_kernel(m_ref, w_ref, x_ref, o_ref, *, tm, rc):
    for r in range(tm // rc):
        r0, r1 = r * rc, (r + 1) * rc
        o_ref[r0:r1, :] = x_ref[r0:r1, :] + jnp.dot(m_ref[r0:r1, :], w_ref[...],
                                                    preferred_element_type=F32)


def _oproj_residual(merged, w_o, x, tm=1024, tn=1024, rc=256):
    b, s, d = merged.shape
    return pl.pallas_call(
        functools.partial(_oproj_kernel, tm=tm, rc=rc),
        grid=(b, s // tm, d // tn),
        in_specs=[pl.BlockSpec((None, tm, d), lambda bi, i, j: (bi, i, 0)),
                  pl.BlockSpec((d, tn), lambda bi, i, j: (0, j)),
                  pl.BlockSpec((None, tm, tn), lambda bi, i, j: (bi, i, j))],
        out_specs=pl.BlockSpec((None, tm, tn), lambda bi, i, j: (bi, i, j)),
        out_shape=jax.ShapeDtypeStruct((b, s, d), F32),
        compiler_params=_params(("parallel", "parallel", "arbitrary")),
        name="oproj_residual",
    )(merged, w_o, x)


def _ffn_kernel(x_ref, gffn_ref, gfin_ref, wg_ref, wu_ref, wd_ref, o_ref, h_ref, acc_ref):
    f = pl.program_id(2)

    @pl.when(f == 0)
    def _():
        x = x_ref[...]
        ms = jnp.mean(x * x, axis=-1, keepdims=True)
        h_ref[...] = (x * lax.rsqrt(ms + EPS) * gffn_ref[...]).astype(h_ref.dtype)
        acc_ref[...] = jnp.zeros(acc_ref.shape, F32)

    h = h_ref[...]
    gate = jnp.dot(h, wg_ref[...], preferred_element_type=F32)
    up = jnp.dot(h, wu_ref[...], preferred_element_type=F32)
    act = (gate / (1.0 + jnp.exp(-gate))) * up
    acc_ref[...] += jnp.dot(act.astype(BF16), wd_ref[...], preferred_element_type=F32)

    @pl.when(f == pl.num_programs(2) - 1)
    def _():
        y = x_ref[...] + acc_ref[...]
        ms = jnp.mean(y * y, axis=-1, keepdims=True)
        o_ref[...] = y * lax.rsqrt(ms + EPS) * gfin_ref[...]


def _ffn_final(x, g_ffn, g_final, w_gu, w_down, tm=512, tf=512):
    b, s, d = x.shape
    dff = w_down.shape[0]
    nf = dff // tf
    return pl.pallas_call(
        _ffn_kernel,
        grid=(b, s // tm, nf),
        in_specs=[pl.BlockSpec((None, tm, d), lambda bi, i, f: (bi, i, 0)),
                  pl.BlockSpec((1, d), lambda bi, i, f: (0, 0)),
                  pl.BlockSpec((1, d), lambda bi, i, f: (0, 0)),
                  pl.BlockSpec((d, tf), lambda bi, i, f: (0, f)),
                  pl.BlockSpec((d, tf), lambda bi, i, f: (0, nf + f)),
                  pl.BlockSpec((tf, d), lambda bi, i, f: (f, 0))],
        out_specs=pl.BlockSpec((None, tm, d), lambda bi, i, f: (bi, i, 0)),
        out_shape=jax.ShapeDtypeStruct((b, s, d), F32),
        scratch_shapes=[pltpu.VMEM((tm, d), BF16), pltpu.VMEM((tm, d), F32)],
        compiler_params=_params(("parallel", "parallel", "arbitrary")),
        name="ffn_final",
    )(x, g_ffn.reshape(1, d).astype(F32), g_final.reshape(1, d).astype(F32), w_gu, w_gu, w_down)


def _rope_tables(s):
    half = HEAD_DIM // 2
    inv = ROPE_THETA ** (-(jnp.arange(half, dtype=F32) * 2.0 / HEAD_DIM))
    ang = jnp.arange(s).astype(F32)[:, None] * inv[None, :]
    cos, sin = jnp.cos(ang), jnp.sin(ang)
    return jnp.concatenate([cos, cos], axis=-1), jnp.concatenate([-sin, sin], axis=-1)


def _axial_tables(s):
    dim = HEAD_DIM // 2
    half = dim // 2
    inv = ROPE_THETA ** (-(jnp.arange(half, dtype=F32) * 2.0 / dim))
    n_rows = s // GRID_W
    row = jnp.repeat(jnp.arange(n_rows), GRID_W).astype(F32)
    col = jnp.tile(jnp.arange(GRID_W), n_rows).astype(F32)
    cs, sn = [], []
    for pos in (row, col):
        ang = pos[:, None] * inv[None, :]
        c, sgn = jnp.cos(ang), jnp.sin(ang)
        cs += [c, c]
        sn += [-sgn, sgn]
    return jnp.concatenate(cs, axis=-1), jnp.concatenate(sn, axis=-1)


def _trunk(x, p):
    b, s, d = x.shape
    a_w = N_HEADS_A * HEAD_DIM
    b_q = d
    b_kv = N_KV_B * HEAD_DIM
    cuts = np.cumsum([0, a_w, a_w, a_w, b_q, b_kv, b_kv, 2 * d]).tolist()
    w_in = p["w_in"]

    def wcols(k):
        return w_in, cuts[k], cuts[k + 1] - cuts[k]

    h = _rmsnorm_bf16(x, p["g_attn"])
    cos1, sin1 = p["rope"]
    cos2, sin2 = p["axial"]

    qa = _proj(h, *wcols(0), kind="rope", out_kind="heads", tables=(cos1 * QSCALE, sin1 * QSCALE),
               name="proj_qa")
    ka = _proj(h, *wcols(1), kind="rope", out_kind="heads", tables=(cos1, sin1), pad=KV_PAD,
               name="proj_ka")
    va = _proj(h, *wcols(2), kind="plain", out_kind="heads", pad=KV_PAD, name="proj_va")
    qb = _proj(h, *wcols(3), kind="qknorm", out_kind="natural", tables=(cos2, sin2),
               gain=p["q_gain"] * QSCALE, name="proj_qb")
    kb = _proj(h, *wcols(4), kind="qknorm", out_kind="heads", tables=(cos2, sin2),
               gain=p["k_gain"], name="proj_kb")
    vbt = _proj(h, *wcols(5), kind="plain", out_kind="vT", rc=DENSE_TK, name="proj_vb")
    gates = _proj(h, *wcols(6), kind="sigmoid", out_kind="natural", out_dtype=F32, name="proj_gates")

    ya = _banded_attention(qa, ka, va)
    yb = _dense_attention(qb, kb, vbt)
    merged = _gated_merge(ya, yb, p["w_a_br"], p["w_b_br"], gates)
    x1 = _oproj_residual(merged, p["w_o"], x)
    return _ffn_final(x1, p["g_ffn"], p["g_final"], p["w_gate_up"], p["w_down"])


def kernel(x_prompt, x_sample, g_attn, w_in, q_gain_b, k_gain_b, w_a_br, w_b_br, w_o, g_ffn,
           w_gate_up, w_down, g_final):
    assert g_attn.shape[0] == 1, "single-layer trunk"
    s = x_prompt.shape[1]
    assert x_sample.shape[1] == s and s % Q_CHUNK == 0
    p = {
        "g_attn": g_attn[0], "g_ffn": g_ffn[0], "g_final": g_final,
        "w_in": w_in[0].astype(BF16),
        "q_gain": q_gain_b[0].reshape(1, HEAD_DIM).astype(F32),
        "k_gain": k_gain_b[0].reshape(1, HEAD_DIM).astype(F32),
        "w_a_br": w_a_br[0].astype(BF16), "w_b_br": w_b_br[0].astype(BF16),
        "w_o": w_o[0].astype(BF16),
        "w_gate_up": w_gate_up[0].astype(BF16), "w_down": w_down[0].astype(BF16),
        "rope": _rope_tables(s), "axial": _axial_tables(s),
    }
    return (_trunk(x_prompt, p), _trunk(x_sample, p))
```

```python
import functools
import math

import numpy as np
import jax
import jax.numpy as jnp
from jax import lax
from jax.experimental import pallas as pl
from jax.experimental.pallas import tpu as pltpu

HEAD_DIM = 128
DILATIONS = (1, 4, 16)
BAND_RADIUS = 64
HEADS_PER_GROUP_A = 8
N_HEADS_A = HEADS_PER_GROUP_A * len(DILATIONS)
N_KV_B = 4
GQA_GROUP = 4
GRID_W = 64
ROPE_THETA = 10000.0
EPS = 1e-6
NEG_BIG = -1e30

LANES = 128
VMEM_BYTES_V7X = 64 * 1024 * 1024
VMEM_LIMIT = VMEM_BYTES_V7X - 8 * 1024 * 1024

F32 = jnp.float32
BF16 = jnp.bfloat16

QSCALE = (HEAD_DIM ** -0.5) * math.log2(math.e)

KV_PAD = BAND_RADIUS * max(DILATIONS)
Q_CHUNK = 2 * KV_PAD
Q_TILE = 256
_HALO = tuple(max(BAND_RADIUS * d, LANES) for d in DILATIONS)
WIN_OFF = tuple(KV_PAD - h for h in _HALO)
WIN_LEN = tuple(Q_TILE + 2 * h for h in _HALO)

DENSE_TQ = 256
DENSE_TK = 1024


def _params(sem, flags=None):
    return pltpu.CompilerParams(dimension_semantics=sem, vmem_limit_bytes=VMEM_LIMIT, flags=flags)


def _rmsnorm_kernel(x_ref, g_ref, o_ref):
    x = x_ref[...]
    ms = jnp.mean(x * x, axis=-1, keepdims=True)
    o_ref[...] = (x * lax.rsqrt(ms + EPS) * g_ref[...]).astype(o_ref.dtype)


def _rmsnorm_bf16(x, g, tm=512):
    b, s, d = x.shape
    return pl.pallas_call(
        _rmsnorm_kernel,
        grid=(b, s // tm),
        in_specs=[pl.BlockSpec((None, tm, d), lambda bi, i: (bi, i, 0)),
                  pl.BlockSpec((1, d), lambda bi, i: (0, 0))],
        out_specs=pl.BlockSpec((None, tm, d), lambda bi, i: (bi, i, 0)),
        out_shape=jax.ShapeDtypeStruct((b, s, d), BF16),
        compiler_params=_params(("parallel", "parallel")),
        name="rmsnorm_bf16",
    )(x, g.reshape(1, d).astype(F32))


def _proj_kernel(*refs, kind, out_kind, nh, rc, tm, pad_tiles, src_tiles):
    h_ref, w_ref = refs[0], refs[1]
    o_ref = refs[-1]
    if kind in ("rope", "qknorm"):
        cos_ref, sin_ref = refs[2], refs[3]
    if kind == "qknorm":
        gain_ref = refs[4]
        lane = lax.broadcasted_iota(jnp.int32, (rc, HEAD_DIM), 1)
        lo = (lane % 64) < 32

    def epilogue(x, c, s):
        if kind == "rope":
            return x * c + pltpu.roll(x, 64, 1) * s
        if kind == "qknorm":
            ms = jnp.mean(x * x, axis=-1, keepdims=True)
            y = x * lax.rsqrt(ms + EPS) * gain_ref[...]
            yr = jnp.where(lo, pltpu.roll(y, 96, 1), pltpu.roll(y, 32, 1))
            return y * c + yr * s
        if kind == "sigmoid":
            return 1.0 / (1.0 + jnp.exp(-x))
        return x

    def compute():
        for r in range(tm // rc):
            r0, r1 = r * rc, (r + 1) * rc
            acc = jnp.dot(h_ref[r0:r1, :], w_ref[...], preferred_element_type=F32)
            c = s = None
            if kind in ("rope", "qknorm"):
                c, s = cos_ref[r0:r1, :], sin_ref[r0:r1, :]
            for j in range(nh):
                y = epilogue(acc[:, j * HEAD_DIM:(j + 1) * HEAD_DIM], c, s)
                if out_kind == "heads":
                    o_ref[j, r0:r1, :] = y.astype(o_ref.dtype)
                elif out_kind == "natural":
                    o_ref[r0:r1, j * HEAD_DIM:(j + 1) * HEAD_DIM] = y.astype(o_ref.dtype)
                else:
                    o_ref[j, r] = y.T.astype(o_ref.dtype)

    if pad_tiles == 0:
        compute()
    else:
        i = pl.program_id(1)
        is_pad = jnp.logical_or(i < pad_tiles, i >= pad_tiles + src_tiles)

        @pl.when(is_pad)
        def _():
            o_ref[...] = jnp.zeros(o_ref.shape, o_ref.dtype)

        @pl.when(jnp.logical_not(is_pad))
        def _():
            compute()


def _proj(h, w, col0, n, *, kind, out_kind, out_dtype=BF16, tables=(), gain=None, pad=0,
          tm=1024, tn=1024, rc=256, name):
    b, s, d = h.shape
    tn = min(tn, n)
    assert col0 % tn == 0 and n % tn == 0
    cb0 = col0 // tn
    nh = tn // HEAD_DIM
    src_tiles = s // tm
    pad_tiles = pad // tm
    rows = s + 2 * pad

    def src(i):
        if pad_tiles == 0:
            return i
        return jnp.clip(i - pad_tiles, 0, src_tiles - 1)

    in_specs = [pl.BlockSpec((None, tm, d), lambda bi, i, j: (bi, src(i), 0)),
                pl.BlockSpec((d, tn), lambda bi, i, j: (0, cb0 + j))]
    args = [h, w]
    for t in tables:
        in_specs.append(pl.BlockSpec((tm, HEAD_DIM), lambda bi, i, j: (src(i), 0)))
        args.append(t)
    if gain is not None:
        in_specs.append(pl.BlockSpec((1, HEAD_DIM), lambda bi, i, j: (0, 0)))
        args.append(gain)

    if out_kind == "heads":
        out_shape = (b, n // HEAD_DIM, rows, HEAD_DIM)
        out_spec = pl.BlockSpec((None, nh, tm, HEAD_DIM), lambda bi, i, j: (bi, j, i, 0))
    elif out_kind == "natural":
        out_shape = (b, rows, n)
        out_spec = pl.BlockSpec((None, tm, tn), lambda bi, i, j: (bi, i, j))
    else:
        out_shape = (b, n // HEAD_DIM, rows // rc, HEAD_DIM, rc)
        out_spec = pl.BlockSpec((None, nh, tm // rc, HEAD_DIM, rc),
                                lambda bi, i, j: (bi, j, i, 0, 0))

    kern = functools.partial(_proj_kernel, kind=kind, out_kind=out_kind, nh=nh, rc=rc, tm=tm,
                             pad_tiles=pad_tiles, src_tiles=src_tiles)
    return pl.pallas_call(
        kern,
        grid=(b, rows // tm, n // tn),
        in_specs=in_specs,
        out_specs=out_spec,
        out_shape=jax.ShapeDtypeStruct(out_shape, out_dtype),
        compiler_params=_params(("parallel", "parallel", "arbitrary")),
        name=name,
    )(*args)


def _band_bias(g):
    d = DILATIONS[g]
    qi = np.arange(Q_TILE)[:, None]
    kj = np.arange(WIN_LEN[g])[None, :]
    rel = kj + WIN_OFF[g] - KV_PAD - qi
    ok = (rel % d == 0) & (np.abs(rel) <= BAND_RADIUS * d)
    return np.where(ok, 0.0, NEG_BIG).astype(np.float32)


def _banded_kernel(*refs, seq):
    ng = len(DILATIONS)
    q_refs = refs[0:ng]
    k_refs = refs[ng:3 * ng]
    v_refs = refs[3 * ng:5 * ng]
    bias_refs = refs[5 * ng:6 * ng]
    o_ref = refs[6 * ng]
    kbuf, vbuf = refs[6 * ng + 1], refs[6 * ng + 2]
    c = pl.program_id(2)

    for g in range(ng):
        kbuf[g, 0:Q_CHUNK, :] = k_refs[2 * g][...]
        kbuf[g, Q_CHUNK:2 * Q_CHUNK, :] = k_refs[2 * g + 1][...]
        vbuf[g, 0:Q_CHUNK, :] = v_refs[2 * g][...]
        vbuf[g, Q_CHUNK:2 * Q_CHUNK, :] = v_refs[2 * g + 1][...]

    def tile(t, carry):
        r0 = pl.multiple_of(t * Q_TILE, Q_TILE)
        m = jnp.full((Q_TILE, 1), NEG_BIG, F32)
        l = jnp.zeros((Q_TILE, 1), F32)
        acc = jnp.zeros((Q_TILE, HEAD_DIM), F32)
        for g in range(ng):
            q = q_refs[g][pl.ds(r0, Q_TILE), :]
            off = pl.multiple_of(r0 + WIN_OFF[g], LANES)
            kw = kbuf[g, pl.ds(off, WIN_LEN[g]), :]
            vw = vbuf[g, pl.ds(off, WIN_LEN[g]), :]
            s = lax.dot_general(q, kw, (((1,), (1,)), ((), ())), preferred_element_type=F32)
            kpos = c * Q_CHUNK + off + lax.broadcasted_iota(jnp.int32, (1, WIN_LEN[g]), 1)
            kvalid = jnp.where((kpos >= KV_PAD) & (kpos < KV_PAD + seq), 0.0, NEG_BIG)
            s = s + bias_refs[g][...] + kvalid
            m_new = jnp.maximum(m, jnp.max(s, axis=-1, keepdims=True))
            alpha = jnp.exp2(m - m_new)
            p = jnp.exp2(s - m_new)
            l = alpha * l + jnp.sum(p, axis=-1, keepdims=True)
            acc = alpha * acc + jnp.dot(p.astype(BF16), vw, preferred_element_type=F32)
            m = m_new
        o_ref[pl.ds(r0, Q_TILE), :] = (acc / l).astype(o_ref.dtype)
        return carry

    lax.fori_loop(0, Q_CHUNK // Q_TILE, tile, 0)


def _banded_attention(qa, ka, va):
    b, _, s, _ = qa.shape
    ng = len(DILATIONS)
    hpg = HEADS_PER_GROUP_A
    blk = (None, None, Q_CHUNK, HEAD_DIM)
    in_specs, args = [], []
    for g in range(ng):
        in_specs.append(pl.BlockSpec(blk, lambda bi, hg, c, g=g: (bi, g * hpg + hg, c, 0)))
        args.append(qa)
    for arr in (ka, va):
        for g in range(ng):
            in_specs.append(pl.BlockSpec(blk, lambda bi, hg, c, g=g: (bi, g * hpg + hg, c, 0)))
            in_specs.append(pl.BlockSpec(blk, lambda bi, hg, c, g=g: (bi, g * hpg + hg, c + 1, 0)))
            args += [arr, arr]
    for g in range(ng):
        in_specs.append(pl.BlockSpec((Q_TILE, WIN_LEN[g]), lambda bi, hg, c: (0, 0)))
        args.append(jnp.asarray(_band_bias(g)))
    return pl.pallas_call(
        functools.partial(_banded_kernel, seq=s),
        grid=(b, hpg, s // Q_CHUNK),
        in_specs=in_specs,
        out_specs=pl.BlockSpec((None, Q_CHUNK, HEAD_DIM), lambda bi, hg, c: (bi, c, hg)),
        out_shape=jax.ShapeDtypeStruct((b, s, hpg * HEAD_DIM), BF16),
        scratch_shapes=[pltpu.VMEM((ng, 2 * Q_CHUNK, HEAD_DIM), BF16),
                        pltpu.VMEM((ng, 2 * Q_CHUNK, HEAD_DIM), BF16)],
        compiler_params=_params(("parallel", "parallel", "arbitrary")),
        name="banded_attention",
    )(*args)


def _dense_kernel(q_ref, k_ref, vt_ref, o_ref, acc_ref, s_ref, mx_ref, *, tq, tk):
    nq = GQA_GROUP * tq
    n_chunks = k_ref.shape[0] // tk
    q = jnp.concatenate([q_ref[:, g * HEAD_DIM:(g + 1) * HEAD_DIM] for g in range(GQA_GROUP)],
                        axis=0)
    acc_ref[...] = jnp.zeros(acc_ref.shape, F32)

    def scores(c, slot):
        k0 = pl.multiple_of(c * tk, tk)
        st = lax.dot_general(k_ref[pl.ds(k0, tk), :], q, (((1,), (1,)), ((), ())),
                             preferred_element_type=F32)
        s_ref[slot] = st
        mx_ref[slot] = jnp.max(st, axis=0, keepdims=True)

    def update(c, slot, m, l):
        m_new = jnp.maximum(m, mx_ref[slot])
        alpha = jnp.exp2(m - m_new)
        p = jnp.exp2(s_ref[slot] - m_new)
        l = alpha * l + jnp.sum(p, axis=0, keepdims=True)
        pv = jnp.dot(vt_ref[c], p.astype(BF16), preferred_element_type=F32)
        acc_ref[...] = acc_ref[...] * alpha + pv
        return m_new, l

    scores(0, 0)

    def pair(j, carry, last):
        m, l = carry
        c0 = 2 * j
        scores(c0 + 1, 1)
        m, l = update(c0, 0, m, l)
        if not last:
            scores(c0 + 2, 0)
        return update(c0 + 1, 1, m, l)

    m0 = jnp.full((1, nq), NEG_BIG, F32)
    l0 = jnp.zeros((1, nq), F32)
    n_pairs = n_chunks // 2
    carry = lax.fori_loop(0, n_pairs - 1, functools.partial(pair, last=False), (m0, l0))
    _, l = pair(n_pairs - 1, carry, last=True)
    o = acc_ref[...] / l
    for g in range(GQA_GROUP):
        o_ref[:, g * HEAD_DIM:(g + 1) * HEAD_DIM] = o[:, g * tq:(g + 1) * tq].T.astype(o_ref.dtype)


def _dense_attention(qb, kb, vbt, tq=DENSE_TQ):
    b, s, dq = qb.shape
    tk = vbt.shape[-1]
    gw = GQA_GROUP * HEAD_DIM
    return pl.pallas_call(
        functools.partial(_dense_kernel, tq=tq, tk=tk),
        grid=(b, N_KV_B, s // tq),
        in_specs=[pl.BlockSpec((None, tq, gw), lambda bi, h, i: (bi, i, h)),
                  pl.BlockSpec((None, None, s, HEAD_DIM), lambda bi, h, i: (bi, h, 0, 0)),
                  pl.BlockSpec((None, None, s // tk, HEAD_DIM, tk),
                               lambda bi, h, i: (bi, h, 0, 0, 0))],
        out_specs=pl.BlockSpec((None, tq, gw), lambda bi, h, i: (bi, i, h)),
        out_shape=jax.ShapeDtypeStruct((b, s, dq), BF16),
        scratch_shapes=[pltpu.VMEM((HEAD_DIM, GQA_GROUP * tq), F32),
                        pltpu.VMEM((2, tk, GQA_GROUP * tq), F32),
                        pltpu.VMEM((2, 1, GQA_GROUP * tq), F32)],
        compiler_params=_params(("parallel", "parallel", "arbitrary")),
        name="dense_gqa_attention",
    )(qb, kb, vbt)


def _merge_kernel(ya_ref, yb_ref, wa_ref, wb_ref, sa_ref, sb_ref, o_ref, *, tm, rc):
    for r in range(tm // rc):
        r0, r1 = r * rc, (r + 1) * rc
        pa = jnp.dot(ya_ref[r0:r1, :], wa_ref[...], preferred_element_type=F32)
        pb = jnp.dot(yb_ref[r0:r1, :], wb_ref[...], preferred_element_type=F32)
        o_ref[r0:r1, :] = (sa_ref[r0:r1, :] * pa + sb_ref[r0:r1, :] * pb).astype(o_ref.dtype)


def _gated_merge(ya, yb, wa, wb, gates, tm=512, tn=1024, rc=256):
    b, s, da = ya.shape
    db = yb.shape[-1]
    d = wa.shape[1]
    nt = d // tn
    return pl.pallas_call(
        functools.partial(_merge_kernel, tm=tm, rc=rc),
        grid=(b, s // tm, nt),
        in_specs=[pl.BlockSpec((None, tm, da), lambda bi, i, j: (bi, i, 0)),
                  pl.BlockSpec((None, tm, db), lambda bi, i, j: (bi, i, 0)),
                  pl.BlockSpec((da, tn), lambda bi, i, j: (0, j)),
                  pl.BlockSpec((db, tn), lambda bi, i, j: (0, j)),
                  pl.BlockSpec((None, tm, tn), lambda bi, i, j: (bi, i, j)),
                  pl.BlockSpec((None, tm, tn), lambda bi, i, j: (bi, i, nt + j))],
        out_specs=pl.BlockSpec((None, tm, tn), lambda bi, i, j: (bi, i, j)),
        out_shape=jax.ShapeDtypeStruct((b, s, d), BF16),
        compiler_params=_params(("parallel", "parallel", "arbitrary")),
        name="gated_merge",
    )(ya, yb, wa, wb, gates, gates)


def _oproj_kernel(m_ref, w_ref, x_ref, o_ref, *, tm, rc):
    for r in range(tm // rc):
        r0, r1 = r * rc, (r + 1) * rc
        o_ref[r0:r1, :] = x_ref[r0:r1, :] + jnp.dot(m_ref[r0:r1, :], w_ref[...],
                                                    preferred_element_type=F32)


def _oproj_residual(merged, w_o, x, tm=1024, tn=1024, rc=256):
    b, s, d = merged.shape
    return pl.pallas_call(
        functools.partial(_oproj_kernel, tm=tm, rc=rc),
        grid=(b, s // tm, d // tn),
        in_specs=[pl.BlockSpec((None, tm, d), lambda bi, i, j: (bi, i, 0)),
                  pl.BlockSpec((d, tn), lambda bi, i, j: (0, j)),
                  pl.BlockSpec((None, tm, tn), lambda bi, i, j: (bi, i, j))],
        out_specs=pl.BlockSpec((None, tm, tn), lambda bi, i, j: (bi, i, j)),
        out_shape=jax.ShapeDtypeStruct((b, s, d), F32),
        compiler_params=_params(("parallel", "parallel", "arbitrary")),
        name="oproj_residual",
    )(merged, w_o, x)


def _ffn_kernel(x_ref, gffn_ref, gfin_ref, wg_ref, wu_ref, wd_ref, o_ref, h_ref, acc_ref):
    f = pl.program_id(2)

    @pl.when(f == 0)
    def _():
        x = x_ref[...]
        ms = jnp.mean(x * x, axis=-1, keepdims=True)
        h_ref[...] = (x * lax.rsqrt(ms + EPS) * gffn_ref[...]).astype(h_ref.dtype)
        acc_ref[...] = jnp.zeros(acc_ref.shape, F32)

    h = h_ref[...]
    gate = jnp.dot(h, wg_ref[...], preferred_element_type=F32)
    up = jnp.dot(h, wu_ref[...], preferred_element_type=F32)
    act = (gate / (1.0 + jnp.exp(-gate))) * up
    acc_ref[...] += jnp.dot(act.astype(BF16), wd_ref[...], preferred_element_type=F32)

    @pl.when(f == pl.num_programs(2) - 1)
    def _():
        y = x_ref[...] + acc_ref[...]
        ms = jnp.mean(y * y, axis=-1, keepdims=True)
        o_ref[...] = y * lax.rsqrt(ms + EPS) * gfin_ref[...]


def _ffn_final(x, g_ffn, g_final, w_gu, w_down, tm=512, tf=512):
    b, s, d = x.shape
    dff = w_down.shape[0]
    nf = dff // tf
    return pl.pallas_call(
        _ffn_kernel,
        grid=(b, s // tm, nf),
        in_specs=[pl.BlockSpec((None, tm, d), lambda bi, i, f: (bi, i, 0)),
                  pl.BlockSpec((1, d), lambda bi, i, f: (0, 0)),
                  pl.BlockSpec((1, d), lambda bi, i, f: (0, 0)),
                  pl.BlockSpec((d, tf), lambda bi, i, f: (0, f)),
                  pl.BlockSpec((d, tf), lambda bi, i, f: (0, nf + f)),
                  pl.BlockSpec((tf, d), lambda bi, i, f: (f, 0))],
        out_specs=pl.BlockSpec((None, tm, d), lambda bi, i, f: (bi, i, 0)),
        out_shape=jax.ShapeDtypeStruct((b, s, d), F32),
        scratch_shapes=[pltpu.VMEM((tm, d), BF16), pltpu.VMEM((tm, d), F32)],
        compiler_params=_params(("parallel", "parallel", "arbitrary")),
        name="ffn_final",
    )(x, g_ffn.reshape(1, d).astype(F32), g_final.reshape(1, d).astype(F32), w_gu, w_gu, w_down)


def _rope_tables(s):
    half = HEAD_DIM // 2
    inv = ROPE_THETA ** (-(jnp.arange(half, dtype=F32) * 2.0 / HEAD_DIM))
    ang = jnp.arange(s).astype(F32)[:, None] * inv[None, :]
    cos, sin = jnp.cos(ang), jnp.sin(ang)
    return jnp.concatenate([cos, cos], axis=-1), jnp.concatenate([-sin, sin], axis=-1)


def _axial_tables(s):
    dim = HEAD_DIM // 2
    half = dim // 2
    inv = ROPE_THETA ** (-(jnp.arange(half, dtype=F32) * 2.0 / dim))
    n_rows = s // GRID_W
    row = jnp.repeat(jnp.arange(n_rows), GRID_W).astype(F32)
    col = jnp.tile(jnp.arange(GRID_W), n_rows).astype(F32)
    cs, sn = [], []
    for pos in (row, col):
        ang = pos[:, None] * inv[None, :]
        c, sgn = jnp.cos(ang), jnp.sin(ang)
        cs += [c, c]
        sn += [-sgn, sgn]
    return jnp.concatenate(cs, axis=-1), jnp.concatenate(sn, axis=-1)


def _trunk(x, p):
    b, s, d = x.shape
    a_w = N_HEADS_A * HEAD_DIM
    b_q = d
    b_kv = N_KV_B * HEAD_DIM
    cuts = np.cumsum([0, a_w, a_w, a_w, b_q, b_kv, b_kv, 2 * d]).tolist()
    w_in = p["w_in"]

    def wcols(k):
        return w_in, cuts[k], cuts[k + 1] - cuts[k]

    h = _rmsnorm_bf16(x, p["g_attn"])
    cos1, sin1 = p["rope"]
    cos2, sin2 = p["axial"]

    qa = _proj(h, *wcols(0), kind="rope", out_kind="heads", tables=(cos1 * QSCALE, sin1 * QSCALE),
               name="proj_qa")
    ka = _proj(h, *wcols(1), kind="rope", out_kind="heads", tables=(cos1, sin1), pad=KV_PAD,
               name="proj_ka")
    va = _proj(h, *wcols(2), kind="plain", out_kind="heads", pad=KV_PAD, name="proj_va")
    qb = _proj(h, *wcols(3), kind="qknorm", out_kind="natural", tables=(cos2, sin2),
               gain=p["q_gain"] * QSCALE, name="proj_qb")
    kb = _proj(h, *wcols(4), kind="qknorm", out_kind="heads", tables=(cos2, sin2),
               gain=p["k_gain"], name="proj_kb")
    vbt = _proj(h, *wcols(5), kind="plain", out_kind="vT", rc=DENSE_TK, name="proj_vb")
    gates = _proj(h, *wcols(6), kind="sigmoid", out_kind="natural", out_dtype=F32, name="proj_gates")

    ya = _banded_attention(qa, ka, va)
    yb = _dense_attention(qb, kb, vbt)
    merged = _gated_merge(ya, yb, p["w_a_br"], p["w_b_br"], gates)
    x1 = _oproj_residual(merged, p["w_o"], x)
    return _ffn_final(x1, p["g_ffn"], p["g_final"], p["w_gate_up"], p["w_down"])


def kernel(x_prompt, x_sample, g_attn, w_in, q_gain_b, k_gain_b, w_a_br, w_b_br, w_o, g_ffn,
           w_gate_up, w_down, g_final):
    assert g_attn.shape[0] == 1, "single-layer trunk"
    s = x_prompt.shape[1]
    assert x_sample.shape[1] == s and s % Q_CHUNK == 0
    p = {
        "g_attn": g_attn[0], "g_ffn": g_ffn[0], "g_final": g_final,
        "w_in": w_in[0].astype(BF16),
        "q_gain": q_gain_b[0].reshape(1, HEAD_DIM).astype(F32),
        "k_gain": k_gain_b[0].reshape(1, HEAD_DIM).astype(F32),
        "w_a_br": w_a_br[0].astype(BF16), "w_b_br": w_b_br[0].astype(BF16),
        "w_o": w_o[0].astype(BF16),
        "w_gate_up": w_gate_up[0].astype(BF16), "w_down": w_down[0].astype(BF16),
        "rope": _rope_tables(s), "axial": _axial_tables(s),
    }
    return (_trunk(x_prompt, p), _trunk(x_sample, p))
```

```python
import functools
import math

import numpy as np
import jax
import jax.numpy as jnp
from jax import lax
from jax.experimental import pallas as pl
from jax.experimental.pallas import tpu as pltpu

HEAD_DIM = 128
DILATIONS = (1, 4, 16)
BAND_RADIUS = 64
HEADS_PER_GROUP_A = 8
N_HEADS_A = HEADS_PER_GROUP_A * len(DILATIONS)
N_KV_B = 4
GQA_GROUP = 4
GRID_W = 64
ROPE_THETA = 10000.0
EPS = 1e-6
NEG_BIG = -1e30

LANES = 128
VMEM_BYTES_V7X = 64 * 1024 * 1024
VMEM_LIMIT = VMEM_BYTES_V7X - 8 * 1024 * 1024

F32 = jnp.float32
BF16 = jnp.bfloat16

QSCALE = (HEAD_DIM ** -0.5) * math.log2(math.e)

KV_PAD = BAND_RADIUS * max(DILATIONS)
Q_CHUNK = 2 * KV_PAD
BAND_TILE = 256

DENSE_TQ = 256
DENSE_TK = 1024


def _params(sem, flags=None):
    return pltpu.CompilerParams(dimension_semantics=sem, vmem_limit_bytes=VMEM_LIMIT, flags=flags)


def _rmsnorm_kernel(x_ref, g_ref, o_ref):
    x = x_ref[...]
    ms = jnp.mean(x * x, axis=-1, keepdims=True)
    o_ref[...] = (x * lax.rsqrt(ms + EPS) * g_ref[...]).astype(o_ref.dtype)


def _rmsnorm_bf16(x, g, tm=512):
    b, s, d = x.shape
    return pl.pallas_call(
        _rmsnorm_kernel,
        grid=(b, s // tm),
        in_specs=[pl.BlockSpec((None, tm, d), lambda bi, i: (bi, i, 0)),
                  pl.BlockSpec((1, d), lambda bi, i: (0, 0))],
        out_specs=pl.BlockSpec((None, tm, d), lambda bi, i: (bi, i, 0)),
        out_shape=jax.ShapeDtypeStruct((b, s, d), BF16),
        compiler_params=_params(("parallel", "parallel")),
        name="rmsnorm_bf16",
    )(x, g.reshape(1, d).astype(F32))


def _proj_kernel(*refs, kind, out_kind, nh, rc, tm, pad_tiles, src_tiles):
    h_ref, w_ref = refs[0], refs[1]
    o_ref = refs[-1]
    if kind in ("rope", "qknorm"):
        cos_ref, sin_ref = refs[2], refs[3]
    if kind == "qknorm":
        gain_ref = refs[4]
        lane = lax.broadcasted_iota(jnp.int32, (rc, HEAD_DIM), 1)
        lo = (lane % 64) < 32

    def epilogue(x, c, s):
        if kind == "rope":
            return x * c + pltpu.roll(x, 64, 1) * s
        if kind == "qknorm":
            ms = jnp.mean(x * x, axis=-1, keepdims=True)
            y = x * lax.rsqrt(ms + EPS) * gain_ref[...]
            yr = jnp.where(lo, pltpu.roll(y, 96, 1), pltpu.roll(y, 32, 1))
            return y * c + yr * s
        if kind == "sigmoid":
            return 1.0 / (1.0 + jnp.exp(-x))
        return x

    def compute():
        for r in range(tm // rc):
            r0, r1 = r * rc, (r + 1) * rc
            acc = jnp.dot(h_ref[r0:r1, :], w_ref[...], preferred_element_type=F32)
            c = s = None
            if kind in ("rope", "qknorm"):
                c, s = cos_ref[r0:r1, :], sin_ref[r0:r1, :]
            for j in range(nh):
                y = epilogue(acc[:, j * HEAD_DIM:(j + 1) * HEAD_DIM], c, s)
                if out_kind == "heads":
                    o_ref[j, r0:r1, :] = y.astype(o_ref.dtype)
                elif out_kind == "natural":
                    o_ref[r0:r1, j * HEAD_DIM:(j + 1) * HEAD_DIM] = y.astype(o_ref.dtype)
                else:
                    o_ref[j, r] = y.T.astype(o_ref.dtype)

    if pad_tiles == 0:
        compute()
    else:
        i = pl.program_id(1)
        is_pad = jnp.logical_or(i < pad_tiles, i >= pad_tiles + src_tiles)

        @pl.when(is_pad)
        def _():
            o_ref[...] = jnp.zeros(o_ref.shape, o_ref.dtype)

        @pl.when(jnp.logical_not(is_pad))
        def _():
            compute()


def _proj(h, w, col0, n, *, kind, out_kind, out_dtype=BF16, tables=(), gain=None, pad=0,
          tm=1024, tn=1024, rc=256, name):
    b, s, d = h.shape
    tn = min(tn, n)
    assert col0 % tn == 0 and n % tn == 0
    cb0 = col0 // tn
    nh = tn // HEAD_DIM
    src_tiles = s // tm
    pad_tiles = pad // tm
    rows = s + 2 * pad

    def src(i):
        if pad_tiles == 0:
            return i
        return jnp.clip(i - pad_tiles, 0, src_tiles - 1)

    in_specs = [pl.BlockSpec((None, tm, d), lambda bi, i, j: (bi, src(i), 0)),
                pl.BlockSpec((d, tn), lambda bi, i, j: (0, cb0 + j))]
    args = [h, w]
    for t in tables:
        in_specs.append(pl.BlockSpec((tm, HEAD_DIM), lambda bi, i, j: (src(i), 0)))
        args.append(t)
    if gain is not None:
        in_specs.append(pl.BlockSpec((1, HEAD_DIM), lambda bi, i, j: (0, 0)))
        args.append(gain)

    if out_kind == "heads":
        out_shape = (b, n // HEAD_DIM, rows, HEAD_DIM)
        out_spec = pl.BlockSpec((None, nh, tm, HEAD_DIM), lambda bi, i, j: (bi, j, i, 0))
    elif out_kind == "natural":
        out_shape = (b, rows, n)
        out_spec = pl.BlockSpec((None, tm, tn), lambda bi, i, j: (bi, i, j))
    else:
        out_shape = (b, n // HEAD_DIM, rows // rc, HEAD_DIM, rc)
        out_spec = pl.BlockSpec((None, nh, tm // rc, HEAD_DIM, rc),
                                lambda bi, i, j: (bi, j, i, 0, 0))

    kern = functools.partial(_proj_kernel, kind=kind, out_kind=out_kind, nh=nh, rc=rc, tm=tm,
                             pad_tiles=pad_tiles, src_tiles=src_tiles)
    return pl.pallas_call(
        kern,
        grid=(b, rows // tm, n // tn),
        in_specs=in_specs,
        out_specs=out_spec,
        out_shape=jax.ShapeDtypeStruct(out_shape, out_dtype),
        compiler_params=_params(("parallel", "parallel", "arbitrary")),
        name=name,
    )(*args)


def _band_bias(t):
    qi = np.arange(t)[:, None]
    kj = np.arange(t + 2 * BAND_RADIUS)[None, :]
    ok = np.abs(kj - BAND_RADIUS - qi) <= BAND_RADIUS
    return np.where(ok, 0.0, NEG_BIG).astype(np.float32)


def _band_kernel(q_ref, ka_ref, kb_ref, va_ref, vb_ref, bias_ref, o_ref, lse_ref, *,
                 d, t, cm, padm, seq_m):
    c = pl.program_id(2)
    win = t + 2 * BAND_RADIUS

    def window(a_ref, b_ref, w0, lanes):
        if w0 + win <= cm:
            return a_ref[w0:w0 + win, lanes]
        if w0 >= cm:
            return b_ref[w0 - cm:w0 - cm + win, lanes]
        return jnp.concatenate([a_ref[w0:cm, lanes], b_ref[0:w0 + win - cm, lanes]], axis=0)

    for u in range(cm // t):
        m0 = u * t
        w0 = m0 + padm - BAND_RADIUS
        kpos = c * cm + w0 + lax.broadcasted_iota(jnp.int32, (1, win), 1)
        kvalid = jnp.where((kpos >= padm) & (kpos < padm + seq_m), 0.0, NEG_BIG)
        bias = bias_ref[...] + kvalid
        for r in range(d):
            lanes = slice(r * HEAD_DIM, (r + 1) * HEAD_DIM)
            q = q_ref[m0:m0 + t, lanes]
            kw = window(ka_ref, kb_ref, w0, lanes)
            vw = window(va_ref, vb_ref, w0, lanes)
            s = lax.dot_general(q, kw, (((1,), (1,)), ((), ())), preferred_element_type=F32)
            s = s + bias
            m = jnp.max(s, axis=-1, keepdims=True)
            p = jnp.exp2(s - m)
            l = jnp.sum(p, axis=-1, keepdims=True)
            o = jnp.dot(p.astype(BF16), vw, preferred_element_type=F32) / l
            o_ref[m0:m0 + t, lanes] = o.astype(o_ref.dtype)
            lse_ref[m0:m0 + t, lanes] = jnp.broadcast_to(m + jnp.log2(l), (t, HEAD_DIM))


def _banded_group(qa, ka, va, g):
    b, nh, s, _ = qa.shape
    d = DILATIONS[g]
    hpg = HEADS_PER_GROUP_A
    w = d * HEAD_DIM
    cm = Q_CHUNK // d
    t = min(BAND_TILE, cm)
    padm = KV_PAD // d
    qv = qa.reshape(b, nh, s // d, w)
    kv = ka.reshape(b, nh, (s + 2 * KV_PAD) // d, w)
    vv = va.reshape(b, nh, (s + 2 * KV_PAD) // d, w)
    blk = (None, None, cm, w)
    head = lambda bi, hg, c: (bi, g * hpg + hg, c, 0)
    nxt = lambda bi, hg, c: (bi, g * hpg + hg, c + 1, 0)
    out = lambda bi, hg, c: (bi, hg, c, 0)
    o, lse = pl.pallas_call(
        functools.partial(_band_kernel, d=d, t=t, cm=cm, padm=padm, seq_m=s // d),
        grid=(b, hpg, s // Q_CHUNK),
        in_specs=[pl.BlockSpec(blk, head), pl.BlockSpec(blk, head), pl.BlockSpec(blk, nxt),
                  pl.BlockSpec(blk, head), pl.BlockSpec(blk, nxt),
                  pl.BlockSpec((t, t + 2 * BAND_RADIUS), lambda bi, hg, c: (0, 0))],
        out_specs=[pl.BlockSpec(blk, out), pl.BlockSpec(blk, out)],
        out_shape=[jax.ShapeDtypeStruct((b, hpg, s // d, w), BF16),
                   jax.ShapeDtypeStruct((b, hpg, s // d, w), F32)],
        compiler_params=_params(("parallel", "parallel", "arbitrary")),
        name=f"banded_attention_d{d}",
    )(qv, kv, kv, vv, vv, jnp.asarray(_band_bias(t)))
    return o.reshape(b, hpg, s, HEAD_DIM), lse.reshape(b, hpg, s, HEAD_DIM)


def _dense_kernel(q_ref, k_ref, vt_ref, o_ref, acc_ref, s_ref, mx_ref, *, tq, tk):
    nq = GQA_GROUP * tq
    n_chunks = k_ref.shape[0] // tk
    q = jnp.concatenate([q_ref[:, g * HEAD_DIM:(g + 1) * HEAD_DIM] for g in range(GQA_GROUP)],
                        axis=0)
    acc_ref[...] = jnp.zeros(acc_ref.shape, F32)

    def scores(c, slot):
        k0 = pl.multiple_of(c * tk, tk)
        st = lax.dot_general(k_ref[pl.ds(k0, tk), :], q, (((1,), (1,)), ((), ())),
                             preferred_element_type=F32)
        s_ref[slot] = st
        mx_ref[slot] = jnp.max(st, axis=0, keepdims=True)

    def update(c, slot, m, l):
        m_new = jnp.maximum(m, mx_ref[slot])
        alpha = jnp.exp2(m - m_new)
        p = jnp.exp2(s_ref[slot] - m_new)
        l = alpha * l + jnp.sum(p, axis=0, keepdims=True)
        pv = jnp.dot(vt_ref[c], p.astype(BF16), preferred_element_type=F32)
        acc_ref[...] = acc_ref[...] * alpha + pv
        return m_new, l

    scores(0, 0)

    def pair(j, carry, last):
        m, l = carry
        c0 = 2 * j
        scores(c0 + 1, 1)
        m, l = update(c0, 0, m, l)
        if not last:
            scores(c0 + 2, 0)
        return update(c0 + 1, 1, m, l)

    m0 = jnp.full((1, nq), NEG_BIG, F32)
    l0 = jnp.zeros((1, nq), F32)
    n_pairs = n_chunks // 2
    carry = lax.fori_loop(0, n_pairs - 1, functools.partial(pair, last=False), (m0, l0))
    _, l = pair(n_pairs - 1, carry, last=True)
    o = acc_ref[...] / l
    for g in range(GQA_GROUP):
        o_ref[:, g * HEAD_DIM:(g + 1) * HEAD_DIM] = o[:, g * tq:(g + 1) * tq].T.astype(o_ref.dtype)


def _dense_attention(qb, kb, vbt, tq=DENSE_TQ):
    b, s, dq = qb.shape
    tk = vbt.shape[-1]
    gw = GQA_GROUP * HEAD_DIM
    return pl.pallas_call(
        functools.partial(_dense_kernel, tq=tq, tk=tk),
        grid=(b, N_KV_B, s // tq),
        in_specs=[pl.BlockSpec((None, tq, gw), lambda bi, h, i: (bi, i, h)),
                  pl.BlockSpec((None, None, s, HEAD_DIM), lambda bi, h, i: (bi, h, 0, 0)),
                  pl.BlockSpec((None, None, s // tk, HEAD_DIM, tk),
                               lambda bi, h, i: (bi, h, 0, 0, 0))],
        out_specs=pl.BlockSpec((None, tq, gw), lambda bi, h, i: (bi, i, h)),
        out_shape=jax.ShapeDtypeStruct((b, s, dq), BF16),
        scratch_shapes=[pltpu.VMEM((HEAD_DIM, GQA_GROUP * tq), F32),
                        pltpu.VMEM((2, tk, GQA_GROUP * tq), F32),
                        pltpu.VMEM((2, 1, GQA_GROUP * tq), F32)],
        compiler_params=_params(("parallel", "parallel", "arbitrary")),
        name="dense_gqa_attention",
    )(qb, kb, vbt)


def _merge_kernel(*refs, tm, rc, ng):
    o_refs, lse_refs = refs[0:ng], refs[ng:2 * ng]
    yb_ref, wa_ref, wb_ref, sa_ref, sb_ref, out_ref, ya_ref = refs[2 * ng:]

    @pl.when(pl.program_id(2) == 0)
    def _():
        for h in range(HEADS_PER_GROUP_A):
            lses = [lse_refs[g][h] for g in range(ng)]
            top = functools.reduce(jnp.maximum, lses)
            es = [jnp.exp2(x - top) for x in lses]
            num = sum(e * o_refs[g][h].astype(F32) for g, e in enumerate(es))
            ya_ref[:, h * HEAD_DIM:(h + 1) * HEAD_DIM] = (num / sum(es)).astype(ya_ref.dtype)

    for r in range(tm // rc):
        r0, r1 = r * rc, (r + 1) * rc
        pa = jnp.dot(ya_ref[r0:r1, :], wa_ref[...], preferred_element_type=F32)
        pb = jnp.dot(yb_ref[r0:r1, :], wb_ref[...], preferred_element_type=F32)
        out_ref[r0:r1, :] = (sa_ref[r0:r1, :] * pa + sb_ref[r0:r1, :] * pb).astype(out_ref.dtype)


def _gated_merge(group_outs, group_lses, yb, wa, wb, gates, tm=512, tn=1024, rc=256):
    ng = len(group_outs)
    b, hpg, s, _ = group_outs[0].shape
    da = hpg * HEAD_DIM
    db = yb.shape[-1]
    d = wa.shape[1]
    nt = d // tn
    hblk = pl.BlockSpec((None, hpg, tm, HEAD_DIM), lambda bi, i, j: (bi, 0, i, 0))
    return pl.pallas_call(
        functools.partial(_merge_kernel, tm=tm, rc=rc, ng=ng),
        grid=(b, s // tm, nt),
        in_specs=[hblk] * (2 * ng) + [
            pl.BlockSpec((None, tm, db), lambda bi, i, j: (bi, i, 0)),
            pl.BlockSpec((da, tn), lambda bi, i, j: (0, j)),
            pl.BlockSpec((db, tn), lambda bi, i, j: (0, j)),
            pl.BlockSpec((None, tm, tn), lambda bi, i, j: (bi, i, j)),
            pl.BlockSpec((None, tm, tn), lambda bi, i, j: (bi, i, nt + j))],
        out_specs=pl.BlockSpec((None, tm, tn), lambda bi, i, j: (bi, i, j)),
        out_shape=jax.ShapeDtypeStruct((b, s, d), BF16),
        scratch_shapes=[pltpu.VMEM((tm, da), BF16)],
        compiler_params=_params(("parallel", "parallel", "arbitrary")),
        name="gated_merge",
    )(*group_outs, *group_lses, yb, wa, wb, gates, gates)


def _oproj_kernel(m_ref, w_ref, x_ref, o_ref, *, tm, rc):
    for r in range(tm // rc):
        r0, r1 = r * rc, (r + 1) * rc
        o_ref[r0:r1, :] = x_ref[r0:r1, :] + jnp.dot(m_ref[r0:r1, :], w_ref[...],
                                                    preferred_element_type=F32)


def _oproj_residual(merged, w_o, x, tm=1024, tn=1024, rc=256):
    b, s, d = merged.shape
    return pl.pallas_call(
        functools.partial(_oproj_kernel, tm=tm, rc=rc),
        grid=(b, s // tm, d // tn),
        in_specs=[pl.BlockSpec((None, tm, d), lambda bi, i, j: (bi, i, 0)),
                  pl.BlockSpec((d, tn), lambda bi, i, j: (0, j)),
                  pl.BlockSpec((None, tm, tn), lambda bi, i, j: (bi, i, j))],
        out_specs=pl.BlockSpec((None, tm, tn), lambda bi, i, j: (bi, i, j)),
        out_shape=jax.ShapeDtypeStruct((b, s, d), F32),
        compiler_params=_params(("parallel", "parallel", "arbitrary")),
        name="oproj_residual",
    )(merged, w_o, x)


def _ffn_kernel(x_ref, gffn_ref, gfin_ref, wg_ref, wu_ref, wd_ref, o_ref, h_ref, acc_ref):
    f = pl.program_id(2)

    @pl.when(f == 0)
    def _():
        x = x_ref[...]
        ms = jnp.mean(x * x, axis=-1, keepdims=True)
        h_ref[...] = (x * lax.rsqrt(ms + EPS) * gffn_ref[...]).astype(h_ref.dtype)
        acc_ref[...] = jnp.zeros(acc_ref.shape, F32)

    h = h_ref[...]
    gate = jnp.dot(h, wg_ref[...], preferred_element_type=F32)
    up = jnp.dot(h, wu_ref[...], preferred_element_type=F32)
    act = (gate / (1.0 + jnp.exp(-gate))) * up
    acc_ref[...] += jnp.dot(act.astype(BF16), wd_ref[...], preferred_element_type=F32)

    @pl.when(f == pl.num_programs(2) - 1)
    def _():
        y = x_ref[...] + acc_ref[...]
        ms = jnp.mean(y * y, axis=-1, keepdims=True)
        o_ref[...] = y * lax.rsqrt(ms + EPS) * gfin_ref[...]


def _ffn_final(x, g_ffn, g_final, w_gu, w_down, tm=512, tf=512):
    b, s, d = x.shape
    dff = w_down.shape[0]
    nf = dff // tf
    return pl.pallas_call(
        _ffn_kernel,
        grid=(b, s // tm, nf),
        in_specs=[pl.BlockSpec((None, tm, d), lambda bi, i, f: (bi, i, 0)),
                  pl.BlockSpec((1, d), lambda bi, i, f: (0, 0)),
                  pl.BlockSpec((1, d), lambda bi, i, f: (0, 0)),
                  pl.BlockSpec((d, tf), lambda bi, i, f: (0, f)),
                  pl.BlockSpec((d, tf), lambda bi, i, f: (0, nf + f)),
                  pl.BlockSpec((tf, d), lambda bi, i, f: (f, 0))],
        out_specs=pl.BlockSpec((None, tm, d), lambda bi, i, f: (bi, i, 0)),
        out_shape=jax.ShapeDtypeStruct((b, s, d), F32),
        scratch_shapes=[pltpu.VMEM((tm, d), BF16), pltpu.VMEM((tm, d), F32)],
        compiler_params=_params(("parallel", "parallel", "arbitrary")),
        name="ffn_final",
    )(x, g_ffn.reshape(1, d).astype(F32), g_final.reshape(1, d).astype(F32), w_gu, w_gu, w_down)


def _rope_tables(s):
    half = HEAD_DIM // 2
    inv = ROPE_THETA ** (-(jnp.arange(half, dtype=F32) * 2.0 / HEAD_DIM))
    ang = jnp.arange(s).astype(F32)[:, None] * inv[None, :]
    cos, sin = jnp.cos(ang), jnp.sin(ang)
    return jnp.concatenate([cos, cos], axis=-1), jnp.concatenate([-sin, sin], axis=-1)


def _axial_tables(s):
    dim = HEAD_DIM // 2
    half = dim // 2
    inv = ROPE_THETA ** (-(jnp.arange(half, dtype=F32) * 2.0 / dim))
    n_rows = s // GRID_W
    row = jnp.repeat(jnp.arange(n_rows), GRID_W).astype(F32)
    col = jnp.tile(jnp.arange(GRID_W), n_rows).astype(F32)
    cs, sn = [], []
    for pos in (row, col):
        ang = pos[:, None] * inv[None, :]
        c, sgn = jnp.cos(ang), jnp.sin(ang)
        cs += [c, c]
        sn += [-sgn, sgn]
    return jnp.concatenate(cs, axis=-1), jnp.concatenate(sn, axis=-1)


def _trunk(x, p):
    b, s, d = x.shape
    a_w = N_HEADS_A * HEAD_DIM
    b_q = d
    b_kv = N_KV_B * HEAD_DIM
    cuts = np.cumsum([0, a_w, a_w, a_w, b_q, b_kv, b_kv, 2 * d]).tolist()
    w_in = p["w_in"]

    def wcols(k):
        return w_in, cuts[k], cuts[k + 1] - cuts[k]

    h = _rmsnorm_bf16(x, p["g_attn"])
    cos1, sin1 = p["rope"]
    cos2, sin2 = p["axial"]

    qa = _proj(h, *wcols(0), kind="rope", out_kind="heads", tables=(cos1 * QSCALE, sin1 * QSCALE),
               name="proj_qa")
    ka = _proj(h, *wcols(1), kind="rope", out_kind="heads", tables=(cos1, sin1), pad=KV_PAD,
               name="proj_ka")
    va = _proj(h, *wcols(2), kind="plain", out_kind="heads", pad=KV_PAD, name="proj_va")
    qb = _proj(h, *wcols(3), kind="qknorm", out_kind="natural", tables=(cos2, sin2),
               gain=p["q_gain"] * QSCALE, name="proj_qb")
    kb = _proj(h, *wcols(4), kind="qknorm", out_kind="heads", tables=(cos2, sin2),
               gain=p["k_gain"], name="proj_kb")
    vbt = _proj(h, *wcols(5), kind="plain", out_kind="vT", rc=DENSE_TK, name="proj_vb")
    gates = _proj(h, *wcols(6), kind="sigmoid", out_kind="natural", out_dtype=F32, name="proj_gates")

    groups = [_banded_group(qa, ka, va, g) for g in range(len(DILATIONS))]
    yb = _dense_attention(qb, kb, vbt)
    merged = _gated_merge([o for o, _ in groups], [l for _, l in groups], yb,
                          p["w_a_br"], p["w_b_br"], gates)
    x1 = _oproj_residual(merged, p["w_o"], x)
    return _ffn_final(x1, p["g_ffn"], p["g_final"], p["w_gate_up"], p["w_down"])


def kernel(x_prompt, x_sample, g_attn, w_in, q_gain_b, k_gain_b, w_a_br, w_b_br, w_o, g_ffn,
           w_gate_up, w_down, g_final):
    assert g_attn.shape[0] == 1, "single-layer trunk"
    s = x_prompt.shape[1]
    assert x_sample.shape[1] == s and s % Q_CHUNK == 0
    p = {
        "g_attn": g_attn[0], "g_ffn": g_ffn[0], "g_final": g_final,
        "w_in": w_in[0].astype(BF16),
        "q_gain": q_gain_b[0].reshape(1, HEAD_DIM).astype(F32),
        "k_gain": k_gain_b[0].reshape(1, HEAD_DIM).astype(F32),
        "w_a_br": w_a_br[0].astype(BF16), "w_b_br": w_b_br[0].astype(BF16),
        "w_o": w_o[0].astype(BF16),
        "w_gate_up": w_gate_up[0].astype(BF16), "w_down": w_down[0].astype(BF16),
        "rope": _rope_tables(s), "axial": _axial_tables(s),
    }
    return (_trunk(x_prompt, p), _trunk(x_sample, p))
```

```python
import functools
import math

import numpy as np
import jax
import jax.numpy as jnp
from jax import lax
from jax.experimental import pallas as pl
from jax.experimental.pallas import tpu as pltpu

HEAD_DIM = 128
DILATIONS = (1, 4, 16)
BAND_RADIUS = 64
HEADS_PER_GROUP_A = 8
N_HEADS_A = HEADS_PER_GROUP_A * len(DILATIONS)
N_KV_B = 4
GQA_GROUP = 4
GRID_W = 64
ROPE_THETA = 10000.0
EPS = 1e-6
NEG_BIG = -1e30

LANES = 128
VMEM_BYTES_V7X = 64 * 1024 * 1024
VMEM_LIMIT = VMEM_BYTES_V7X - 8 * 1024 * 1024

F32 = jnp.float32
BF16 = jnp.bfloat16

QSCALE = (HEAD_DIM ** -0.5) * math.log2(math.e)

KV_PAD = BAND_RADIUS * max(DILATIONS)
Q_CHUNK = 2 * KV_PAD
BAND_TILE = 256

DENSE_TQ = 256
DENSE_TK = 1024


def _params(sem, flags=None):
    return pltpu.CompilerParams(dimension_semantics=sem, vmem_limit_bytes=VMEM_LIMIT, flags=flags)


def _rmsnorm_kernel(x_ref, g_ref, o_ref):
    x = x_ref[...]
    ms = jnp.mean(x * x, axis=-1, keepdims=True)
    o_ref[...] = (x * lax.rsqrt(ms + EPS) * g_ref[...]).astype(o_ref.dtype)


def _rmsnorm_bf16(x, g, tm=512):
    b, s, d = x.shape
    return pl.pallas_call(
        _rmsnorm_kernel,
        grid=(b, s // tm),
        in_specs=[pl.BlockSpec((None, tm, d), lambda bi, i: (bi, i, 0)),
                  pl.BlockSpec((1, d), lambda bi, i: (0, 0))],
        out_specs=pl.BlockSpec((None, tm, d), lambda bi, i: (bi, i, 0)),
        out_shape=jax.ShapeDtypeStruct((b, s, d), BF16),
        compiler_params=_params(("parallel", "parallel")),
        name="rmsnorm_bf16",
    )(x, g.reshape(1, d).astype(F32))


def _proj_kernel(*refs, kind, out_kind, nh, rc, tm, pad_tiles, src_tiles):
    h_ref, w_ref = refs[0], refs[1]
    o_ref = refs[-1]
    if kind in ("rope", "qknorm"):
        cos_ref, sin_ref = refs[2], refs[3]
    if kind == "qknorm":
        gain_ref = refs[4]
        lane = lax.broadcasted_iota(jnp.int32, (rc, HEAD_DIM), 1)
        lo = (lane % 64) < 32

    def epilogue(x, c, s):
        if kind == "rope":
            return x * c + pltpu.roll(x, 64, 1) * s
        if kind == "qknorm":
            ms = jnp.mean(x * x, axis=-1, keepdims=True)
            y = x * lax.rsqrt(ms + EPS) * gain_ref[...]
            yr = jnp.where(lo, pltpu.roll(y, 96, 1), pltpu.roll(y, 32, 1))
            return y * c + yr * s
        if kind == "sigmoid":
            return 1.0 / (1.0 + jnp.exp(-x))
        return x

    def compute():
        for r in range(tm // rc):
            r0, r1 = r * rc, (r + 1) * rc
            acc = jnp.dot(h_ref[r0:r1, :], w_ref[...], preferred_element_type=F32)
            c = s = None
            if kind in ("rope", "qknorm"):
                c, s = cos_ref[r0:r1, :], sin_ref[r0:r1, :]
            for j in range(nh):
                y = epilogue(acc[:, j * HEAD_DIM:(j + 1) * HEAD_DIM], c, s)
                if out_kind == "heads":
                    o_ref[j, r0:r1, :] = y.astype(o_ref.dtype)
                elif out_kind == "natural":
                    o_ref[r0:r1, j * HEAD_DIM:(j + 1) * HEAD_DIM] = y.astype(o_ref.dtype)
                else:
                    o_ref[j, r] = y.T.astype(o_ref.dtype)

    if pad_tiles == 0:
        compute()
    else:
        i = pl.program_id(1)
        is_pad = jnp.logical_or(i < pad_tiles, i >= pad_tiles + src_tiles)

        @pl.when(is_pad)
        def _():
            o_ref[...] = jnp.zeros(o_ref.shape, o_ref.dtype)

        @pl.when(jnp.logical_not(is_pad))
        def _():
            compute()


def _proj(h, w, col0, n, *, kind, out_kind, out_dtype=BF16, tables=(), gain=None, pad=0,
          tm=1024, tn=1024, rc=256, name):
    b, s, d = h.shape
    tn = min(tn, n)
    assert col0 % tn == 0 and n % tn == 0
    cb0 = col0 // tn
    nh = tn // HEAD_DIM
    src_tiles = s // tm
    pad_tiles = pad // tm
    rows = s + 2 * pad

    def src(i):
        if pad_tiles == 0:
            return i
        return jnp.clip(i - pad_tiles, 0, src_tiles - 1)

    in_specs = [pl.BlockSpec((None, tm, d), lambda bi, i, j: (bi, src(i), 0)),
                pl.BlockSpec((d, tn), lambda bi, i, j: (0, cb0 + j))]
    args = [h, w]
    for t in tables:
        in_specs.append(pl.BlockSpec((tm, HEAD_DIM), lambda bi, i, j: (src(i), 0)))
        args.append(t)
    if gain is not None:
        in_specs.append(pl.BlockSpec((1, HEAD_DIM), lambda bi, i, j: (0, 0)))
        args.append(gain)

    if out_kind == "heads":
        out_shape = (b, n // HEAD_DIM, rows, HEAD_DIM)
        out_spec = pl.BlockSpec((None, nh, tm, HEAD_DIM), lambda bi, i, j: (bi, j, i, 0))
    elif out_kind == "natural":
        out_shape = (b, rows, n)
        out_spec = pl.BlockSpec((None, tm, tn), lambda bi, i, j: (bi, i, j))
    else:
        out_shape = (b, n // HEAD_DIM, rows // rc, HEAD_DIM, rc)
        out_spec = pl.BlockSpec((None, nh, tm // rc, HEAD_DIM, rc),
                                lambda bi, i, j: (bi, j, i, 0, 0))

    kern = functools.partial(_proj_kernel, kind=kind, out_kind=out_kind, nh=nh, rc=rc, tm=tm,
                             pad_tiles=pad_tiles, src_tiles=src_tiles)
    return pl.pallas_call(
        kern,
        grid=(b, rows // tm, n // tn),
        in_specs=in_specs,
        out_specs=out_spec,
        out_shape=jax.ShapeDtypeStruct(out_shape, out_dtype),
        compiler_params=_params(("parallel", "parallel", "arbitrary")),
        name=name,
    )(*args)


def _band_bias(t):
    qi = np.arange(t)[:, None]
    kj = np.arange(t + 2 * BAND_RADIUS)[None, :]
    ok = np.abs(kj - BAND_RADIUS - qi) <= BAND_RADIUS
    return np.where(ok, 0.0, NEG_BIG).astype(np.float32)


def _band_kernel(q_ref, kp_ref, kc_ref, kn_ref, vp_ref, vc_ref, vn_ref, bias_ref,
                 o_ref, lse_ref, *scratch, d, t, cm, seq_m):
    c = pl.program_id(2)
    rad = BAND_RADIUS
    win = t + 2 * rad

    def rows(ref, r, n):
        if d == 1:
            return ref[...]
        return ref[pl.ds(r, n, stride=d), :].astype(BF16)

    biases = []
    for u in range(cm // t):
        kpos = c * cm + u * t - rad + lax.broadcasted_iota(jnp.int32, (1, win), 1)
        kvalid = jnp.where((kpos >= 0) & (kpos < seq_m), 0.0, NEG_BIG)
        biases.append(bias_ref[...] + kvalid)

    for r in range(d):
        q = rows(q_ref, r, cm)
        kw = jnp.concatenate([rows(kp_ref, r, rad), rows(kc_ref, r, cm), rows(kn_ref, r, rad)], 0)
        vw = jnp.concatenate([rows(vp_ref, r, rad), rows(vc_ref, r, cm), rows(vn_ref, r, rad)], 0)
        for u in range(cm // t):
            m0 = u * t
            s = lax.dot_general(q[m0:m0 + t], kw[m0:m0 + win], (((1,), (1,)), ((), ())),
                                preferred_element_type=F32) + biases[u]
            m = jnp.max(s, axis=-1, keepdims=True)
            p = jnp.exp2(s - m)
            l = jnp.sum(p, axis=-1, keepdims=True)
            o = jnp.dot(p.astype(BF16), vw[m0:m0 + win], preferred_element_type=F32) / l
            lse = jnp.broadcast_to(m + jnp.log2(l), (t, HEAD_DIM))
            if d == 1:
                o_ref[m0:m0 + t, :] = o.astype(o_ref.dtype)
                lse_ref[m0:m0 + t, :] = lse
            else:
                o_scr, lse_scr = scratch
                o_scr[pl.ds(r + d * m0, t, stride=d), :] = o
                lse_scr[pl.ds(r + d * m0, t, stride=d), :] = lse
    if d > 1:
        o_ref[...] = scratch[0][...].astype(o_ref.dtype)
        lse_ref[...] = scratch[1][...]


def _banded_group(q, k, v, g, head0):
    b, _, s, _ = q.shape
    d = DILATIONS[g]
    hpg = HEADS_PER_GROUP_A
    cm = Q_CHUNK // d
    t = min(BAND_TILE, cm)
    halo = BAND_RADIUS * d
    per = Q_CHUNK // halo
    last = s // halo - 1
    blk = pl.BlockSpec((None, None, Q_CHUNK, HEAD_DIM), lambda bi, hg, c: (bi, head0 + hg, c, 0))
    prev = pl.BlockSpec((None, None, halo, HEAD_DIM),
                        lambda bi, hg, c: (bi, head0 + hg, jnp.maximum(c * per - 1, 0), 0))
    nxt = pl.BlockSpec((None, None, halo, HEAD_DIM),
                       lambda bi, hg, c: (bi, head0 + hg, jnp.minimum((c + 1) * per, last), 0))
    out = pl.BlockSpec((None, None, Q_CHUNK, HEAD_DIM), lambda bi, hg, c: (bi, hg, c, 0))
    scratch = [] if d == 1 else [pltpu.VMEM((Q_CHUNK, HEAD_DIM), F32)] * 2
    return pl.pallas_call(
        functools.partial(_band_kernel, d=d, t=t, cm=cm, seq_m=s // d),
        grid=(b, hpg, s // Q_CHUNK),
        in_specs=[blk, prev, blk, nxt, prev, blk, nxt,
                  pl.BlockSpec((t, t + 2 * BAND_RADIUS), lambda bi, hg, c: (0, 0))],
        out_specs=[out, out],
        out_shape=[jax.ShapeDtypeStruct((b, hpg, s, HEAD_DIM), BF16),
                   jax.ShapeDtypeStruct((b, hpg, s, HEAD_DIM), F32)],
        scratch_shapes=scratch,
        compiler_params=_params(("parallel", "parallel", "arbitrary")),
        name=f"banded_attention_d{d}",
    )(q, k, k, k, v, v, v, jnp.asarray(_band_bias(t)))


def _dense_kernel(q_ref, k_ref, vt_ref, o_ref, acc_ref, s_ref, mx_ref, *, tq, tk):
    nq = GQA_GROUP * tq
    n_chunks = k_ref.shape[0] // tk
    q = jnp.concatenate([q_ref[:, g * HEAD_DIM:(g + 1) * HEAD_DIM] for g in range(GQA_GROUP)],
                        axis=0)
    acc_ref[...] = jnp.zeros(acc_ref.shape, F32)

    def scores(c, slot):
        k0 = pl.multiple_of(c * tk, tk)
        st = lax.dot_general(k_ref[pl.ds(k0, tk), :], q, (((1,), (1,)), ((), ())),
                             preferred_element_type=F32)
        s_ref[slot] = st
        mx_ref[slot] = jnp.max(st, axis=0, keepdims=True)

    def update(c, slot, m, l):
        m_new = jnp.maximum(m, mx_ref[slot])
        alpha = jnp.exp2(m - m_new)
        p = jnp.exp2(s_ref[slot] - m_new)
        l = alpha * l + jnp.sum(p, axis=0, keepdims=True)
        pv = jnp.dot(vt_ref[c], p.astype(BF16), preferred_element_type=F32)
        acc_ref[...] = acc_ref[...] * alpha + pv
        return m_new, l

    scores(0, 0)

    def pair(j, carry, last):
        m, l = carry
        c0 = 2 * j
        scores(c0 + 1, 1)
        m, l = update(c0, 0, m, l)
        if not last:
            scores(c0 + 2, 0)
        return update(c0 + 1, 1, m, l)

    m0 = jnp.full((1, nq), NEG_BIG, F32)
    l0 = jnp.zeros((1, nq), F32)
    n_pairs = n_chunks // 2
    carry = lax.fori_loop(0, n_pairs - 1, functools.partial(pair, last=False), (m0, l0))
    _, l = pair(n_pairs - 1, carry, last=True)
    o = acc_ref[...] / l
    for g in range(GQA_GROUP):
        o_ref[:, g * HEAD_DIM:(g + 1) * HEAD_DIM] = o[:, g * tq:(g + 1) * tq].T.astype(o_ref.dtype)


def _dense_attention(qb, kb, vbt, tq=DENSE_TQ):
    b, s, dq = qb.shape
    tk = vbt.shape[-1]
    gw = GQA_GROUP * HEAD_DIM
    return pl.pallas_call(
        functools.partial(_dense_kernel, tq=tq, tk=tk),
        grid=(b, N_KV_B, s // tq),
        in_specs=[pl.BlockSpec((None, tq, gw), lambda bi, h, i: (bi, i, h)),
                  pl.BlockSpec((None, None, s, HEAD_DIM), lambda bi, h, i: (bi, h, 0, 0)),
                  pl.BlockSpec((None, None, s // tk, HEAD_DIM, tk),
                               lambda bi, h, i: (bi, h, 0, 0, 0))],
        out_specs=pl.BlockSpec((None, tq, gw), lambda bi, h, i: (bi, i, h)),
        out_shape=jax.ShapeDtypeStruct((b, s, dq), BF16),
        scratch_shapes=[pltpu.VMEM((HEAD_DIM, GQA_GROUP * tq), F32),
                        pltpu.VMEM((2, tk, GQA_GROUP * tq), F32),
                        pltpu.VMEM((2, 1, GQA_GROUP * tq), F32)],
        compiler_params=_params(("parallel", "parallel", "arbitrary")),
        name="dense_gqa_attention",
    )(qb, kb, vbt)


def _merge_kernel(*refs, tm, rc, ng):
    o_refs, lse_refs = refs[0:ng], refs[ng:2 * ng]
    yb_ref, wa_ref, wb_ref, sa_ref, sb_ref, out_ref, ya_ref = refs[2 * ng:]

    @pl.when(pl.program_id(2) == 0)
    def _():
        for h in range(HEADS_PER_GROUP_A):
            lses = [lse_refs[g][h] for g in range(ng)]
            top = functools.reduce(jnp.maximum, lses)
            es = [jnp.exp2(x - top) for x in lses]
            num = sum(e * o_refs[g][h].astype(F32) for g, e in enumerate(es))
            ya_ref[:, h * HEAD_DIM:(h + 1) * HEAD_DIM] = (num / sum(es)).astype(ya_ref.dtype)

    for r in range(tm // rc):
        r0, r1 = r * rc, (r + 1) * rc
        pa = jnp.dot(ya_ref[r0:r1, :], wa_ref[...], preferred_element_type=F32)
        pb = jnp.dot(yb_ref[r0:r1, :], wb_ref[...], preferred_element_type=F32)
        out_ref[r0:r1, :] = (sa_ref[r0:r1, :] * pa + sb_ref[r0:r1, :] * pb).astype(out_ref.dtype)


def _gated_merge(group_outs, group_lses, yb, wa, wb, gates, tm=512, tn=1024, rc=256):
    ng = len(group_outs)
    b, hpg, s, _ = group_outs[0].shape
    da = hpg * HEAD_DIM
    db = yb.shape[-1]
    d = wa.shape[1]
    nt = d // tn
    hblk = pl.BlockSpec((None, hpg, tm, HEAD_DIM), lambda bi, i, j: (bi, 0, i, 0))
    return pl.pallas_call(
        functools.partial(_merge_kernel, tm=tm, rc=rc, ng=ng),
        grid=(b, s // tm, nt),
        in_specs=[hblk] * (2 * ng) + [
            pl.BlockSpec((None, tm, db), lambda bi, i, j: (bi, i, 0)),
            pl.BlockSpec((da, tn), lambda bi, i, j: (0, j)),
            pl.BlockSpec((db, tn), lambda bi, i, j: (0, j)),
            pl.BlockSpec((None, tm, tn), lambda bi, i, j: (bi, i, j)),
            pl.BlockSpec((None, tm, tn), lambda bi, i, j: (bi, i, nt + j))],
        out_specs=pl.BlockSpec((None, tm, tn), lambda bi, i, j: (bi, i, j)),
        out_shape=jax.ShapeDtypeStruct((b, s, d), BF16),
        scratch_shapes=[pltpu.VMEM((tm, da), BF16)],
        compiler_params=_params(("parallel", "parallel", "arbitrary")),
        name="gated_merge",
    )(*group_outs, *group_lses, yb, wa, wb, gates, gates)


def _oproj_kernel(m_ref, w_ref, x_ref, o_ref, *, tm, rc):
    for r in range(tm // rc):
        r0, r1 = r * rc, (r + 1) * rc
        o_ref[r0:r1, :] = x_ref[r0:r1, :] + jnp.dot(m_ref[r0:r1, :], w_ref[...],
                                                    preferred_element_type=F32)


def _oproj_residual(merged, w_o, x, tm=1024, tn=1024, rc=256):
    b, s, d = merged.shape
    return pl.pallas_call(
        functools.partial(_oproj_kernel, tm=tm, rc=rc),
        grid=(b, s // tm, d // tn),
        in_specs=[pl.BlockSpec((None, tm, d), lambda bi, i, j: (bi, i, 0)),
                  pl.BlockSpec((d, tn), lambda bi, i, j: (0, j)),
                  pl.BlockSpec((None, tm, tn), lambda bi, i, j: (bi, i, j))],
        out_specs=pl.BlockSpec((None, tm, tn), lambda bi, i, j: (bi, i, j)),
        out_shape=jax.ShapeDtypeStruct((b, s, d), F32),
        compiler_params=_params(("parallel", "parallel", "arbitrary")),
        name="oproj_residual",
    )(merged, w_o, x)


def _ffn_kernel(x_ref, gffn_ref, gfin_ref, wg_ref, wu_ref, wd_ref, o_ref, h_ref, acc_ref):
    f = pl.program_id(2)

    @pl.when(f == 0)
    def _():
        x = x_ref[...]
        ms = jnp.mean(x * x, axis=-1, keepdims=True)
        h_ref[...] = (x * lax.rsqrt(ms + EPS) * gffn_ref[...]).astype(h_ref.dtype)
        acc_ref[...] = jnp.zeros(acc_ref.shape, F32)

    h = h_ref[...]
    gate = jnp.dot(h, wg_ref[...], preferred_element_type=F32)
    up = jnp.dot(h, wu_ref[...], preferred_element_type=F32)
    act = (gate / (1.0 + jnp.exp(-gate))) * up
    acc_ref[...] += jnp.dot(act.astype(BF16), wd_ref[...], preferred_element_type=F32)

    @pl.when(f == pl.num_programs(2) - 1)
    def _():
        y = x_ref[...] + acc_ref[...]
        ms = jnp.mean(y * y, axis=-1, keepdims=True)
        o_ref[...] = y * lax.rsqrt(ms + EPS) * gfin_ref[...]


def _ffn_final(x, g_ffn, g_final, w_gu, w_down, tm=512, tf=512):
    b, s, d = x.shape
    dff = w_down.shape[0]
    nf = dff // tf
    return pl.pallas_call(
        _ffn_kernel,
        grid=(b, s // tm, nf),
        in_specs=[pl.BlockSpec((None, tm, d), lambda bi, i, f: (bi, i, 0)),
                  pl.BlockSpec((1, d), lambda bi, i, f: (0, 0)),
                  pl.BlockSpec((1, d), lambda bi, i, f: (0, 0)),
                  pl.BlockSpec((d, tf), lambda bi, i, f: (0, f)),
                  pl.BlockSpec((d, tf), lambda bi, i, f: (0, nf + f)),
                  pl.BlockSpec((tf, d), lambda bi, i, f: (f, 0))],
        out_specs=pl.BlockSpec((None, tm, d), lambda bi, i, f: (bi, i, 0)),
        out_shape=jax.ShapeDtypeStruct((b, s, d), F32),
        scratch_shapes=[pltpu.VMEM((tm, d), BF16), pltpu.VMEM((tm, d), F32)],
        compiler_params=_params(("parallel", "parallel", "arbitrary")),
        name="ffn_final",
    )(x, g_ffn.reshape(1, d).astype(F32), g_final.reshape(1, d).astype(F32), w_gu, w_gu, w_down)


def _rope_tables(s):
    half = HEAD_DIM // 2
    inv = ROPE_THETA ** (-(jnp.arange(half, dtype=F32) * 2.0 / HEAD_DIM))
    ang = jnp.arange(s).astype(F32)[:, None] * inv[None, :]
    cos, sin = jnp.cos(ang), jnp.sin(ang)
    return jnp.concatenate([cos, cos], axis=-1), jnp.concatenate([-sin, sin], axis=-1)


def _axial_tables(s):
    dim = HEAD_DIM // 2
    half = dim // 2
    inv = ROPE_THETA ** (-(jnp.arange(half, dtype=F32) * 2.0 / dim))
    n_rows = s // GRID_W
    row = jnp.repeat(jnp.arange(n_rows), GRID_W).astype(F32)
    col = jnp.tile(jnp.arange(GRID_W), n_rows).astype(F32)
    cs, sn = [], []
    for pos in (row, col):
        ang = pos[:, None] * inv[None, :]
        c, sgn = jnp.cos(ang), jnp.sin(ang)
        cs += [c, c]
        sn += [-sgn, sgn]
    return jnp.concatenate(cs, axis=-1), jnp.concatenate(sn, axis=-1)


def _trunk(x, p):
    b, s, d = x.shape
    a_w = N_HEADS_A * HEAD_DIM
    b_q = d
    b_kv = N_KV_B * HEAD_DIM
    cuts = np.cumsum([0, a_w, a_w, a_w, b_q, b_kv, b_kv, 2 * d]).tolist()
    w_in = p["w_in"]

    def wcols(k):
        return w_in, cuts[k], cuts[k + 1] - cuts[k]

    h = _rmsnorm_bf16(x, p["g_attn"])
    cos1, sin1 = p["rope"]
    cos2, sin2 = p["axial"]

    gw = HEADS_PER_GROUP_A * HEAD_DIM
    qkv_a = []
    for col0, n, dt, tag in ((0, gw, BF16, "g0"), (gw, a_w - gw, F32, "g12")):
        qkv_a.append((
            _proj(h, w_in, cuts[0] + col0, n, kind="rope", out_kind="heads", out_dtype=dt,
                  tables=(cos1 * QSCALE, sin1 * QSCALE), name="proj_qa_" + tag),
            _proj(h, w_in, cuts[1] + col0, n, kind="rope", out_kind="heads", out_dtype=dt,
                  tables=(cos1, sin1), name="proj_ka_" + tag),
            _proj(h, w_in, cuts[2] + col0, n, kind="plain", out_kind="heads", out_dtype=dt,
                  name="proj_va_" + tag)))
    qb = _proj(h, *wcols(3), kind="qknorm", out_kind="natural", tables=(cos2, sin2),
               gain=p["q_gain"] * QSCALE, name="proj_qb")
    kb = _proj(h, *wcols(4), kind="qknorm", out_kind="heads", tables=(cos2, sin2),
               gain=p["k_gain"], name="proj_kb")
    vbt = _proj(h, *wcols(5), kind="plain", out_kind="vT", rc=DENSE_TK, name="proj_vb")
    gates = _proj(h, *wcols(6), kind="sigmoid", out_kind="natural", out_dtype=F32, name="proj_gates")

    groups = [_banded_group(*qkv_a[0], 0, 0)]
    groups += [_banded_group(*qkv_a[1], g, (g - 1) * HEADS_PER_GROUP_A)
               for g in range(1, len(DILATIONS))]
    yb = _dense_attention(qb, kb, vbt)
    merged = _gated_merge([o for o, _ in groups], [l for _, l in groups], yb,
                          p["w_a_br"], p["w_b_br"], gates)
    x1 = _oproj_residual(merged, p["w_o"], x)
    return _ffn_final(x1, p["g_ffn"], p["g_final"], p["w_gate_up"], p["w_down"])


def kernel(x_prompt, x_sample, g_attn, w_in, q_gain_b, k_gain_b, w_a_br, w_b_br, w_o, g_ffn,
           w_gate_up, w_down, g_final):
    assert g_attn.shape[0] == 1, "single-layer trunk"
    s = x_prompt.shape[1]
    assert x_sample.shape[1] == s and s % Q_CHUNK == 0
    p = {
        "g_attn": g_attn[0], "g_ffn": g_ffn[0], "g_final": g_final,
        "w_in": w_in[0].astype(BF16),
        "q_gain": q_gain_b[0].reshape(1, HEAD_DIM).astype(F32),
        "k_gain": k_gain_b[0].reshape(1, HEAD_DIM).astype(F32),
        "w_a_br": w_a_br[0].astype(BF16), "w_b_br": w_b_br[0].astype(BF16),
        "w_o": w_o[0].astype(BF16),
        "w_gate_up": w_gate_up[0].astype(BF16), "w_down": w_down[0].astype(BF16),
        "rope": _rope_tables(s), "axial": _axial_tables(s),
    }
    return (_trunk(x_prompt, p), _trunk(x_sample, p))
```

```python
import functools
import math

import numpy as np
import jax
import jax.numpy as jnp
from jax import lax
from jax.experimental import pallas as pl
from jax.experimental.pallas import tpu as pltpu

HEAD_DIM = 128
DILATIONS = (1, 4, 16)
BAND_RADIUS = 64
HEADS_PER_GROUP_A = 8
N_HEADS_A = HEADS_PER_GROUP_A * len(DILATIONS)
N_KV_B = 4
GQA_GROUP = 4
GRID_W = 64
ROPE_THETA = 10000.0
EPS = 1e-6
NEG_BIG = -1e30

LANES = 128
VMEM_BYTES_V7X = 64 * 1024 * 1024
VMEM_LIMIT = VMEM_BYTES_V7X - 8 * 1024 * 1024
FFN_VMEM_LIMIT = VMEM_BYTES_V7X - 3 * 1024 * 1024

F32 = jnp.float32
BF16 = jnp.bfloat16

QSCALE = (HEAD_DIM ** -0.5) * math.log2(math.e)

KV_PAD = BAND_RADIUS * max(DILATIONS)
Q_CHUNK = 2 * KV_PAD
BAND_TILE = 256

DENSE_TQ = 256
DENSE_TK = 1024
SUM_ROWS = 16


def _params(sem, vmem_limit=VMEM_LIMIT):
    return pltpu.CompilerParams(dimension_semantics=sem, vmem_limit_bytes=vmem_limit)


def _rmsnorm_kernel(x_ref, g_ref, o_ref):
    x = x_ref[...]
    ms = jnp.mean(x * x, axis=-1, keepdims=True)
    o_ref[...] = (x * lax.rsqrt(ms + EPS) * g_ref[...]).astype(o_ref.dtype)


def _rmsnorm_bf16(x, g, tm=512):
    b, s, d = x.shape
    return pl.pallas_call(
        _rmsnorm_kernel,
        grid=(b, s // tm),
        in_specs=[pl.BlockSpec((None, tm, d), lambda bi, i: (bi, i, 0)),
                  pl.BlockSpec((1, d), lambda bi, i: (0, 0))],
        out_specs=pl.BlockSpec((None, tm, d), lambda bi, i: (bi, i, 0)),
        out_shape=jax.ShapeDtypeStruct((b, s, d), BF16),
        compiler_params=_params(("parallel", "parallel")),
        name="rmsnorm_bf16",
    )(x, g.reshape(1, d).astype(F32))


def _proj_kernel(*refs, kind, out_kind, nh, rc, tm, pad_tiles, src_tiles):
    h_ref, w_ref = refs[0], refs[1]
    o_ref = refs[-1]
    if kind in ("rope", "qknorm"):
        cos_ref, sin_ref = refs[2], refs[3]
    if kind == "qknorm":
        gain_ref = refs[4]
        lane = lax.broadcasted_iota(jnp.int32, (rc, HEAD_DIM), 1)
        lo = (lane % 64) < 32

    def epilogue(x, c, s):
        if kind == "rope":
            return x * c + pltpu.roll(x, 64, 1) * s
        if kind == "qknorm":
            ms = jnp.mean(x * x, axis=-1, keepdims=True)
            y = x * lax.rsqrt(ms + EPS) * gain_ref[...]
            yr = jnp.where(lo, pltpu.roll(y, 96, 1), pltpu.roll(y, 32, 1))
            return y * c + yr * s
        if kind == "sigmoid":
            return 1.0 / (1.0 + jnp.exp(-x))
        return x

    def compute():
        for r in range(tm // rc):
            r0, r1 = r * rc, (r + 1) * rc
            acc = jnp.dot(h_ref[r0:r1, :], w_ref[...], preferred_element_type=F32)
            c = s = None
            if kind in ("rope", "qknorm"):
                c, s = cos_ref[r0:r1, :], sin_ref[r0:r1, :]
            for j in range(nh):
                y = epilogue(acc[:, j * HEAD_DIM:(j + 1) * HEAD_DIM], c, s)
                if out_kind == "heads":
                    o_ref[j, r0:r1, :] = y.astype(o_ref.dtype)
                elif out_kind == "natural":
                    o_ref[r0:r1, j * HEAD_DIM:(j + 1) * HEAD_DIM] = y.astype(o_ref.dtype)
                else:
                    o_ref[j, r] = y.T.astype(o_ref.dtype)

    if pad_tiles == 0:
        compute()
    else:
        i = pl.program_id(1)
        is_pad = jnp.logical_or(i < pad_tiles, i >= pad_tiles + src_tiles)

        @pl.when(is_pad)
        def _():
            o_ref[...] = jnp.zeros(o_ref.shape, o_ref.dtype)

        @pl.when(jnp.logical_not(is_pad))
        def _():
            compute()


def _proj(h, w, col0, n, *, kind, out_kind, out_dtype=BF16, tables=(), gain=None, pad=0,
          tm=1024, tn=1024, rc=256, name):
    b, s, d = h.shape
    tn = min(tn, n)
    assert col0 % tn == 0 and n % tn == 0
    cb0 = col0 // tn
    nh = tn // HEAD_DIM
    src_tiles = s // tm
    pad_tiles = pad // tm
    rows = s + 2 * pad

    def src(i):
        if pad_tiles == 0:
            return i
        return jnp.clip(i - pad_tiles, 0, src_tiles - 1)

    in_specs = [pl.BlockSpec((None, tm, d), lambda bi, i, j: (bi, src(i), 0)),
                pl.BlockSpec((d, tn), lambda bi, i, j: (0, cb0 + j))]
    args = [h, w]
    for t in tables:
        in_specs.append(pl.BlockSpec((tm, HEAD_DIM), lambda bi, i, j: (src(i), 0)))
        args.append(t)
    if gain is not None:
        in_specs.append(pl.BlockSpec((1, HEAD_DIM), lambda bi, i, j: (0, 0)))
        args.append(gain)

    if out_kind == "heads":
        out_shape = (b, n // HEAD_DIM, rows, HEAD_DIM)
        out_spec = pl.BlockSpec((None, nh, tm, HEAD_DIM), lambda bi, i, j: (bi, j, i, 0))
    elif out_kind == "natural":
        out_shape = (b, rows, n)
        out_spec = pl.BlockSpec((None, tm, tn), lambda bi, i, j: (bi, i, j))
    else:
        out_shape = (b, n // HEAD_DIM, rows // rc, HEAD_DIM, rc)
        out_spec = pl.BlockSpec((None, nh, tm // rc, HEAD_DIM, rc),
                                lambda bi, i, j: (bi, j, i, 0, 0))

    kern = functools.partial(_proj_kernel, kind=kind, out_kind=out_kind, nh=nh, rc=rc, tm=tm,
                             pad_tiles=pad_tiles, src_tiles=src_tiles)
    return pl.pallas_call(
        kern,
        grid=(b, rows // tm, n // tn),
        in_specs=in_specs,
        out_specs=out_spec,
        out_shape=jax.ShapeDtypeStruct(out_shape, out_dtype),
        compiler_params=_params(("parallel", "parallel", "arbitrary")),
        name=name,
    )(*args)


def _band_bias(t):
    qi = np.arange(t)[:, None]
    kj = np.arange(t + 2 * BAND_RADIUS)[None, :]
    ok = np.abs(kj - BAND_RADIUS - qi) <= BAND_RADIUS
    return np.where(ok, 0.0, NEG_BIG).astype(np.float32)


def _band_kernel(q_ref, kp_ref, kc_ref, kn_ref, vp_ref, vc_ref, vn_ref, bias_ref,
                 o_ref, lse_ref, *scratch, d, t, cm, seq_m):
    c = pl.program_id(2)
    rad = BAND_RADIUS
    win = t + 2 * rad

    def rows(ref, r, n):
        if d == 1:
            return ref[...]
        return ref[pl.ds(r, n, stride=d), :].astype(BF16)

    biases = []
    for u in range(cm // t):
        kpos = c * cm + u * t - rad + lax.broadcasted_iota(jnp.int32, (1, win), 1)
        kvalid = jnp.where((kpos >= 0) & (kpos < seq_m), 0.0, NEG_BIG)
        biases.append(bias_ref[...] + kvalid)

    for r in range(d):
        q = rows(q_ref, r, cm)
        kw = jnp.concatenate([rows(kp_ref, r, rad), rows(kc_ref, r, cm), rows(kn_ref, r, rad)], 0)
        vw = jnp.concatenate([rows(vp_ref, r, rad), rows(vc_ref, r, cm), rows(vn_ref, r, rad)], 0)
        for u in range(cm // t):
            m0 = u * t
            s = lax.dot_general(q[m0:m0 + t], kw[m0:m0 + win], (((1,), (1,)), ((), ())),
                                preferred_element_type=F32) + biases[u]
            m = jnp.max(s, axis=-1, keepdims=True)
            p = jnp.exp2(s - m)
            l = jnp.sum(p, axis=-1, keepdims=True)
            o = jnp.dot(p.astype(BF16), vw[m0:m0 + win], preferred_element_type=F32) / l
            lse = jnp.broadcast_to(m + jnp.log2(l), (t, HEAD_DIM))
            if d == 1:
                o_ref[m0:m0 + t, :] = o.astype(o_ref.dtype)
                lse_ref[m0:m0 + t, :] = lse
            else:
                o_scr, lse_scr = scratch
                o_scr[pl.ds(r + d * m0, t, stride=d), :] = o
                lse_scr[pl.ds(r + d * m0, t, stride=d), :] = lse
    if d > 1:
        o_ref[...] = scratch[0][...].astype(o_ref.dtype)
        lse_ref[...] = scratch[1][...]


def _banded_group(q, k, v, g, head0):
    b, _, s, _ = q.shape
    d = DILATIONS[g]
    hpg = HEADS_PER_GROUP_A
    cm = Q_CHUNK // d
    t = min(BAND_TILE, cm)
    halo = BAND_RADIUS * d
    per = Q_CHUNK // halo
    last = s // halo - 1
    blk = pl.BlockSpec((None, None, Q_CHUNK, HEAD_DIM), lambda bi, hg, c: (bi, head0 + hg, c, 0))
    prev = pl.BlockSpec((None, None, halo, HEAD_DIM),
                        lambda bi, hg, c: (bi, head0 + hg, jnp.maximum(c * per - 1, 0), 0))
    nxt = pl.BlockSpec((None, None, halo, HEAD_DIM),
                       lambda bi, hg, c: (bi, head0 + hg, jnp.minimum((c + 1) * per, last), 0))
    out = pl.BlockSpec((None, None, Q_CHUNK, HEAD_DIM), lambda bi, hg, c: (bi, hg, c, 0))
    scratch = [] if d == 1 else [pltpu.VMEM((Q_CHUNK, HEAD_DIM), F32)] * 2
    return pl.pallas_call(
        functools.partial(_band_kernel, d=d, t=t, cm=cm, seq_m=s // d),
        grid=(b, hpg, s // Q_CHUNK),
        in_specs=[blk, prev, blk, nxt, prev, blk, nxt,
                  pl.BlockSpec((t, t + 2 * BAND_RADIUS), lambda bi, hg, c: (0, 0))],
        out_specs=[out, out],
        out_shape=[jax.ShapeDtypeStruct((b, hpg, s, HEAD_DIM), BF16),
                   jax.ShapeDtypeStruct((b, hpg, s, HEAD_DIM), F32)],
        scratch_shapes=scratch,
        compiler_params=_params(("parallel", "parallel", "arbitrary")),
        name=f"banded_attention_d{d}",
    )(q, k, k, k, v, v, v, jnp.asarray(_band_bias(t)))


def _dense_kernel(q_ref, k_ref, vt_ref, o_ref, acc_ref, s_ref, mx_ref, *, tq, tk):
    nq = GQA_GROUP * tq
    n_chunks = k_ref.shape[0] // tk
    q = jnp.concatenate([q_ref[:, g * HEAD_DIM:(g + 1) * HEAD_DIM] for g in range(GQA_GROUP)],
                        axis=0)
    acc_ref[...] = jnp.zeros(acc_ref.shape, F32)

    def scores(c, slot):
        k0 = pl.multiple_of(c * tk, tk)
        st = lax.dot_general(k_ref[pl.ds(k0, tk), :], q, (((1,), (1,)), ((), ())),
                             preferred_element_type=F32)
        s_ref[slot] = st
        mx_ref[slot] = jnp.max(st, axis=0, keepdims=True)

    ones_rows = jnp.ones((SUM_ROWS, tk), BF16)

    def update(c, slot, m):
        m_new = jnp.maximum(m, mx_ref[slot])
        alpha = jnp.exp2(m - m_new)
        p = jnp.exp2(s_ref[slot] - m_new).astype(BF16)
        vt1 = jnp.concatenate([vt_ref[c], ones_rows], axis=0)
        pv = jnp.dot(vt1, p, preferred_element_type=F32)
        acc_ref[...] = acc_ref[...] * alpha + pv
        return m_new

    scores(0, 0)

    def pair(j, m, last):
        c0 = 2 * j
        scores(c0 + 1, 1)
        m = update(c0, 0, m)
        if not last:
            scores(c0 + 2, 0)
        return update(c0 + 1, 1, m)

    m0 = jnp.full((1, nq), NEG_BIG, F32)
    n_pairs = n_chunks // 2
    m = lax.fori_loop(0, n_pairs - 1, functools.partial(pair, last=False), m0)
    pair(n_pairs - 1, m, last=True)
    o = acc_ref[0:HEAD_DIM, :] / acc_ref[HEAD_DIM:HEAD_DIM + 1, :]
    for g in range(GQA_GROUP):
        o_ref[:, g * HEAD_DIM:(g + 1) * HEAD_DIM] = o[:, g * tq:(g + 1) * tq].T.astype(o_ref.dtype)


def _dense_attention(qb, kb, vbt, tq=DENSE_TQ):
    b, s, dq = qb.shape
    tk = vbt.shape[-1]
    gw = GQA_GROUP * HEAD_DIM
    return pl.pallas_call(
        functools.partial(_dense_kernel, tq=tq, tk=tk),
        grid=(b, N_KV_B, s // tq),
        in_specs=[pl.BlockSpec((None, tq, gw), lambda bi, h, i: (bi, i, h)),
                  pl.BlockSpec((None, None, s, HEAD_DIM), lambda bi, h, i: (bi, h, 0, 0)),
                  pl.BlockSpec((None, None, s // tk, HEAD_DIM, tk),
                               lambda bi, h, i: (bi, h, 0, 0, 0))],
        out_specs=pl.BlockSpec((None, tq, gw), lambda bi, h, i: (bi, i, h)),
        out_shape=jax.ShapeDtypeStruct((b, s, dq), BF16),
        scratch_shapes=[pltpu.VMEM((HEAD_DIM + SUM_ROWS, GQA_GROUP * tq), F32),
                        pltpu.VMEM((2, tk, GQA_GROUP * tq), F32),
                        pltpu.VMEM((2, 1, GQA_GROUP * tq), F32)],
        compiler_params=_params(("parallel", "parallel", "arbitrary")),
        name="dense_gqa_attention",
    )(qb, kb, vbt)


def _merge_kernel(*refs, tm, rc, ng):
    o_refs, lse_refs = refs[0:ng], refs[ng:2 * ng]
    yb_ref, wa_ref, wb_ref, sa_ref, sb_ref, out_ref, ya_ref = refs[2 * ng:]

    @pl.when(pl.program_id(2) == 0)
    def _():
        for h in range(HEADS_PER_GROUP_A):
            lses = [lse_refs[g][h] for g in range(ng)]
            top = functools.reduce(jnp.maximum, lses)
            es = [jnp.exp2(x - top) for x in lses]
            num = sum(e * o_refs[g][h].astype(F32) for g, e in enumerate(es))
            ya_ref[:, h * HEAD_DIM:(h + 1) * HEAD_DIM] = (num / sum(es)).astype(ya_ref.dtype)

    for r in range(tm // rc):
        r0, r1 = r * rc, (r + 1) * rc
        pa = jnp.dot(ya_ref[r0:r1, :], wa_ref[...], preferred_element_type=F32)
        pb = jnp.dot(yb_ref[r0:r1, :], wb_ref[...], preferred_element_type=F32)
        out_ref[r0:r1, :] = (sa_ref[r0:r1, :] * pa + sb_ref[r0:r1, :] * pb).astype(out_ref.dtype)


def _gated_merge(group_outs, group_lses, yb, wa, wb, gates, tm=512, tn=1024, rc=256):
    ng = len(group_outs)
    b, hpg, s, _ = group_outs[0].shape
    da = hpg * HEAD_DIM
    db = yb.shape[-1]
    d = wa.shape[1]
    nt = d // tn
    hblk = pl.BlockSpec((None, hpg, tm, HEAD_DIM), lambda bi, i, j: (bi, 0, i, 0))
    return pl.pallas_call(
        functools.partial(_merge_kernel, tm=tm, rc=rc, ng=ng),
        grid=(b, s // tm, nt),
        in_specs=[hblk] * (2 * ng) + [
            pl.BlockSpec((None, tm, db), lambda bi, i, j: (bi, i, 0)),
            pl.BlockSpec((da, tn), lambda bi, i, j: (0, j)),
            pl.BlockSpec((db, tn), lambda bi, i, j: (0, j)),
            pl.BlockSpec((None, tm, tn), lambda bi, i, j: (bi, i, j)),
            pl.BlockSpec((None, tm, tn), lambda bi, i, j: (bi, i, nt + j))],
        out_specs=pl.BlockSpec((None, tm, tn), lambda bi, i, j: (bi, i, j)),
        out_shape=jax.ShapeDtypeStruct((b, s, d), BF16),
        scratch_shapes=[pltpu.VMEM((tm, da), BF16)],
        compiler_params=_params(("parallel", "parallel", "arbitrary")),
        name="gated_merge",
    )(*group_outs, *group_lses, yb, wa, wb, gates, gates)


def _oproj_kernel(m_ref, w_ref, x_ref, o_ref, *, tm, rc):
    for r in range(tm // rc):
        r0, r1 = r * rc, (r + 1) * rc
        o_ref[r0:r1, :] = x_ref[r0:r1, :] + jnp.dot(m_ref[r0:r1, :], w_ref[...],
                                                    preferred_element_type=F32)


def _oproj_residual(merged, w_o, x, tm=1024, tn=1024, rc=256):
    b, s, d = merged.shape
    return pl.pallas_call(
        functools.partial(_oproj_kernel, tm=tm, rc=rc),
        grid=(b, s // tm, d // tn),
        in_specs=[pl.BlockSpec((None, tm, d), lambda bi, i, j: (bi, i, 0)),
                  pl.BlockSpec((d, tn), lambda bi, i, j: (0, j)),
                  pl.BlockSpec((None, tm, tn), lambda bi, i, j: (bi, i, j))],
        out_specs=pl.BlockSpec((None, tm, tn), lambda bi, i, j: (bi, i, j)),
        out_shape=jax.ShapeDtypeStruct((b, s, d), F32),
        compiler_params=_params(("parallel", "parallel", "arbitrary")),
        name="oproj_residual",
    )(merged, w_o, x)


def _ffn_kernel(x_ref, gffn_ref, gfin_ref, wg_ref, wu_ref, wd_ref, o_ref, h_ref, *, tm, rc):
    f = pl.program_id(2)

    @pl.when(f == 0)
    def _():
        x = x_ref[...]
        ms = jnp.mean(x * x, axis=-1, keepdims=True)
        h_ref[...] = (x * lax.rsqrt(ms + EPS) * gffn_ref[...]).astype(h_ref.dtype)
        o_ref[...] = jnp.zeros(o_ref.shape, F32)

    for r in range(tm // rc):
        r0, r1 = r * rc, (r + 1) * rc
        h = h_ref[r0:r1, :]
        gate = jnp.dot(h, wg_ref[...], preferred_element_type=F32)
        up = jnp.dot(h, wu_ref[...], preferred_element_type=F32)
        act = (gate / (1.0 + jnp.exp(-gate))) * up
        o_ref[r0:r1, :] += jnp.dot(act.astype(BF16), wd_ref[...], preferred_element_type=F32)

    @pl.when(f == pl.num_programs(2) - 1)
    def _():
        y = x_ref[...] + o_ref[...]
        ms = jnp.mean(y * y, axis=-1, keepdims=True)
        o_ref[...] = y * lax.rsqrt(ms + EPS) * gfin_ref[...]


def _ffn_final(x, g_ffn, g_final, w_gu, w_down, tm=1024, tf=512, rc=512):
    b, s, d = x.shape
    dff = w_down.shape[0]
    nf = dff // tf
    return pl.pallas_call(
        functools.partial(_ffn_kernel, tm=tm, rc=rc),
        grid=(b, s // tm, nf),
        in_specs=[pl.BlockSpec((None, tm, d), lambda bi, i, f: (bi, i, 0)),
                  pl.BlockSpec((1, d), lambda bi, i, f: (0, 0)),
                  pl.BlockSpec((1, d), lambda bi, i, f: (0, 0)),
                  pl.BlockSpec((d, tf), lambda bi, i, f: (0, f)),
                  pl.BlockSpec((d, tf), lambda bi, i, f: (0, nf + f)),
                  pl.BlockSpec((tf, d), lambda bi, i, f: (f, 0))],
        out_specs=pl.BlockSpec((None, tm, d), lambda bi, i, f: (bi, i, 0)),
        out_shape=jax.ShapeDtypeStruct((b, s, d), F32),
        scratch_shapes=[pltpu.VMEM((tm, d), BF16)],
        compiler_params=_params(("parallel", "parallel", "arbitrary"), vmem_limit=FFN_VMEM_LIMIT),
        name="ffn_final",
    )(x, g_ffn.reshape(1, d).astype(F32), g_final.reshape(1, d).astype(F32), w_gu, w_gu, w_down)


def _rope_tables(s):
    half = HEAD_DIM // 2
    inv = ROPE_THETA ** (-(jnp.arange(half, dtype=F32) * 2.0 / HEAD_DIM))
    ang = jnp.arange(s).astype(F32)[:, None] * inv[None, :]
    cos, sin = jnp.cos(ang), jnp.sin(ang)
    return jnp.concatenate([cos, cos], axis=-1), jnp.concatenate([-sin, sin], axis=-1)


def _axial_tables(s):
    dim = HEAD_DIM // 2
    half = dim // 2
    inv = ROPE_THETA ** (-(jnp.arange(half, dtype=F32) * 2.0 / dim))
    n_rows = s // GRID_W
    row = jnp.repeat(jnp.arange(n_rows), GRID_W).astype(F32)
    col = jnp.tile(jnp.arange(GRID_W), n_rows).astype(F32)
    cs, sn = [], []
    for pos in (row, col):
        ang = pos[:, None] * inv[None, :]
        c, sgn = jnp.cos(ang), jnp.sin(ang)
        cs += [c, c]
        sn += [-sgn, sgn]
    return jnp.concatenate(cs, axis=-1), jnp.concatenate(sn, axis=-1)


def _trunk(x, p):
    b, s, d = x.shape
    a_w = N_HEADS_A * HEAD_DIM
    b_q = d
    b_kv = N_KV_B * HEAD_DIM
    cuts = np.cumsum([0, a_w, a_w, a_w, b_q, b_kv, b_kv, 2 * d]).tolist()
    w_in = p["w_in"]

    def wcols(k):
        return w_in, cuts[k], cuts[k + 1] - cuts[k]

    h = _rmsnorm_bf16(x, p["g_attn"])
    cos1, sin1 = p["rope"]
    cos2, sin2 = p["axial"]

    gw = HEADS_PER_GROUP_A * HEAD_DIM
    qkv_a = []
    for col0, n, dt, tag in ((0, gw, BF16, "g0"), (gw, a_w - gw, F32, "g12")):
        qkv_a.append((
            _proj(h, w_in, cuts[0] + col0, n, kind="rope", out_kind="heads", out_dtype=dt,
                  tables=(cos1 * QSCALE, sin1 * QSCALE), name="proj_qa_" + tag),
            _proj(h, w_in, cuts[1] + col0, n, kind="rope", out_kind="heads", out_dtype=dt,
                  tables=(cos1, sin1), name="proj_ka_" + tag),
            _proj(h, w_in, cuts[2] + col0, n, kind="plain", out_kind="heads", out_dtype=dt,
                  name="proj_va_" + tag)))
    qb = _proj(h, *wcols(3), kind="qknorm", out_kind="natural", tables=(cos2, sin2),
               gain=p["q_gain"] * QSCALE, name="proj_qb")
    kb = _proj(h, *wcols(4), kind="qknorm", out_kind="heads", tables=(cos2, sin2),
               gain=p["k_gain"], name="proj_kb")
    vbt = _proj(h, *wcols(5), kind="plain", out_kind="vT", rc=DENSE_TK, name="proj_vb")
    gates = _proj(h, *wcols(6), kind="sigmoid", out_kind="natural", out_dtype=F32, name="proj_gates")

    groups = [_banded_group(*qkv_a[0], 0, 0)]
    groups += [_banded_group(*qkv_a[1], g, (g - 1) * HEADS_PER_GROUP_A)
               for g in range(1, len(DILATIONS))]
    yb = _dense_attention(qb, kb, vbt)
    merged = _gated_merge([o for o, _ in groups], [l for _, l in groups], yb,
                          p["w_a_br"], p["w_b_br"], gates)
    x1 = _oproj_residual(merged, p["w_o"], x)
    return _ffn_final(x1, p["g_ffn"], p["g_final"], p["w_gate_up"], p["w_down"])


def kernel(x_prompt, x_sample, g_attn, w_in, q_gain_b, k_gain_b, w_a_br, w_b_br, w_o, g_ffn,
           w_gate_up, w_down, g_final):
    assert g_attn.shape[0] == 1, "single-layer trunk"
    s = x_prompt.shape[1]
    assert x_sample.shape[1] == s and s % Q_CHUNK == 0
    p = {
        "g_attn": g_attn[0], "g_ffn": g_ffn[0], "g_final": g_final,
        "w_in": w_in[0].astype(BF16),
        "q_gain": q_gain_b[0].reshape(1, HEAD_DIM).astype(F32),
        "k_gain": k_gain_b[0].reshape(1, HEAD_DIM).astype(F32),
        "w_a_br": w_a_br[0].astype(BF16), "w_b_br": w_b_br[0].astype(BF16),
        "w_o": w_o[0].astype(BF16),
        "w_gate_up": w_gate_up[0].astype(BF16), "w_down": w_down[0].astype(BF16),
        "rope": _rope_tables(s), "axial": _axial_tables(s),
    }
    return (_trunk(x_prompt, p), _trunk(x_sample, p))
```

```python
import functools
import math

import numpy as np
import jax
import jax.numpy as jnp
from jax import lax
from jax.experimental import pallas as pl
from jax.experimental.pallas import tpu as pltpu

HEAD_DIM = 128
DILATIONS = (1, 4, 16)
BAND_RADIUS = 64
HEADS_PER_GROUP_A = 8
N_HEADS_A = HEADS_PER_GROUP_A * len(DILATIONS)
N_KV_B = 4
GQA_GROUP = 4
GRID_W = 64
ROPE_THETA = 10000.0
EPS = 1e-6
NEG_BIG = -1e30

LANES = 128
VMEM_BYTES_V7X = 64 * 1024 * 1024
VMEM_LIMIT = VMEM_BYTES_V7X - 8 * 1024 * 1024
FFN_VMEM_LIMIT = VMEM_BYTES_V7X - 3 * 1024 * 1024

F32 = jnp.float32
BF16 = jnp.bfloat16

QSCALE = (HEAD_DIM ** -0.5) * math.log2(math.e)

KV_PAD = BAND_RADIUS * max(DILATIONS)
Q_CHUNK = 2 * KV_PAD
BAND_TILE = 256

DENSE_TQ = 256
DENSE_TK = 1024
SUM_ROWS = 16


def _params(sem, vmem_limit=VMEM_LIMIT):
    return pltpu.CompilerParams(dimension_semantics=sem, vmem_limit_bytes=vmem_limit)


def _rmsnorm_kernel(x_ref, g_ref, o_ref):
    x = x_ref[...]
    ms = jnp.mean(x * x, axis=-1, keepdims=True)
    o_ref[...] = (x * lax.rsqrt(ms + EPS) * g_ref[...]).astype(o_ref.dtype)


def _rmsnorm_bf16(x, g, tm=512):
    b, s, d = x.shape
    return pl.pallas_call(
        _rmsnorm_kernel,
        grid=(b, s // tm),
        in_specs=[pl.BlockSpec((None, tm, d), lambda bi, i: (bi, i, 0)),
                  pl.BlockSpec((1, d), lambda bi, i: (0, 0))],
        out_specs=pl.BlockSpec((None, tm, d), lambda bi, i: (bi, i, 0)),
        out_shape=jax.ShapeDtypeStruct((b, s, d), BF16),
        compiler_params=_params(("parallel", "parallel")),
        name="rmsnorm_bf16",
    )(x, g.reshape(1, d).astype(F32))


def _proj_kernel(*refs, kind, out_kind, nh, rc, tm, pad_tiles, src_tiles):
    h_ref, w_ref = refs[0], refs[1]
    o_ref = refs[-1]
    if kind in ("rope", "qknorm"):
        cos_ref, sin_ref = refs[2], refs[3]
    if kind == "qknorm":
        gain_ref = refs[4]
        lane = lax.broadcasted_iota(jnp.int32, (rc, HEAD_DIM), 1)
        lo = (lane % 64) < 32

    def epilogue(x, c, s):
        if kind == "rope":
            return x * c + pltpu.roll(x, 64, 1) * s
        if kind == "qknorm":
            ms = jnp.mean(x * x, axis=-1, keepdims=True)
            y = x * lax.rsqrt(ms + EPS) * gain_ref[...]
            yr = jnp.where(lo, pltpu.roll(y, 96, 1), pltpu.roll(y, 32, 1))
            return y * c + yr * s
        if kind == "sigmoid":
            return 1.0 / (1.0 + jnp.exp(-x))
        return x

    def compute():
        for r in range(tm // rc):
            r0, r1 = r * rc, (r + 1) * rc
            acc = jnp.dot(h_ref[r0:r1, :], w_ref[...], preferred_element_type=F32)
            c = s = None
            if kind in ("rope", "qknorm"):
                c, s = cos_ref[r0:r1, :], sin_ref[r0:r1, :]
            for j in range(nh):
                y = epilogue(acc[:, j * HEAD_DIM:(j + 1) * HEAD_DIM], c, s)
                if out_kind == "heads":
                    o_ref[j, r0:r1, :] = y.astype(o_ref.dtype)
                elif out_kind == "natural":
                    o_ref[r0:r1, j * HEAD_DIM:(j + 1) * HEAD_DIM] = y.astype(o_ref.dtype)
                else:
                    o_ref[j, r] = y.T.astype(o_ref.dtype)

    if pad_tiles == 0:
        compute()
    else:
        i = pl.program_id(1)
        is_pad = jnp.logical_or(i < pad_tiles, i >= pad_tiles + src_tiles)

        @pl.when(is_pad)
        def _():
            o_ref[...] = jnp.zeros(o_ref.shape, o_ref.dtype)

        @pl.when(jnp.logical_not(is_pad))
        def _():
            compute()


def _proj(h, w, col0, n, *, kind, out_kind, out_dtype=BF16, tables=(), gain=None, pad=0,
          tm=1024, tn=1024, rc=256, name):
    b, s, d = h.shape
    tn = min(tn, n)
    assert col0 % tn == 0 and n % tn == 0
    cb0 = col0 // tn
    nh = tn // HEAD_DIM
    src_tiles = s // tm
    pad_tiles = pad // tm
    rows = s + 2 * pad

    def src(i):
        if pad_tiles == 0:
            return i
        return jnp.clip(i - pad_tiles, 0, src_tiles - 1)

    in_specs = [pl.BlockSpec((None, tm, d), lambda bi, i, j: (bi, src(i), 0)),
                pl.BlockSpec((d, tn), lambda bi, i, j: (0, cb0 + j))]
    args = [h, w]
    for t in tables:
        in_specs.append(pl.BlockSpec((tm, HEAD_DIM), lambda bi, i, j: (src(i), 0)))
        args.append(t)
    if gain is not None:
        in_specs.append(pl.BlockSpec((1, HEAD_DIM), lambda bi, i, j: (0, 0)))
        args.append(gain)

    if out_kind == "heads":
        out_shape = (b, n // HEAD_DIM, rows, HEAD_DIM)
        out_spec = pl.BlockSpec((None, nh, tm, HEAD_DIM), lambda bi, i, j: (bi, j, i, 0))
    elif out_kind == "natural":
        out_shape = (b, rows, n)
        out_spec = pl.BlockSpec((None, tm, tn), lambda bi, i, j: (bi, i, j))
    else:
        out_shape = (b, n // HEAD_DIM, rows // rc, HEAD_DIM, rc)
        out_spec = pl.BlockSpec((None, nh, tm // rc, HEAD_DIM, rc),
                                lambda bi, i, j: (bi, j, i, 0, 0))

    kern = functools.partial(_proj_kernel, kind=kind, out_kind=out_kind, nh=nh, rc=rc, tm=tm,
                             pad_tiles=pad_tiles, src_tiles=src_tiles)
    return pl.pallas_call(
        kern,
        grid=(b, rows // tm, n // tn),
        in_specs=in_specs,
        out_specs=out_spec,
        out_shape=jax.ShapeDtypeStruct(out_shape, out_dtype),
        compiler_params=_params(("parallel", "parallel", "arbitrary")),
        name=name,
    )(*args)


def _band_bias(t):
    qi = np.arange(t)[:, None]
    kj = np.arange(t + 2 * BAND_RADIUS)[None, :]
    ok = np.abs(kj - BAND_RADIUS - qi) <= BAND_RADIUS
    return np.where(ok, 0.0, NEG_BIG).astype(np.float32)


def _band_kernel(q_ref, kp_ref, kc_ref, kn_ref, vp_ref, vc_ref, vn_ref, bias_ref,
                 o_ref, lse_ref, *scratch, d, t, cm, seq_m):
    c = pl.program_id(2)
    rad = BAND_RADIUS
    win = t + 2 * rad

    def rows(ref, r, n):
        if d == 1:
            return ref[...]
        return ref[pl.ds(r, n, stride=d), :].astype(BF16)

    biases = []
    for u in range(cm // t):
        kpos = c * cm + u * t - rad + lax.broadcasted_iota(jnp.int32, (1, win), 1)
        kvalid = jnp.where((kpos >= 0) & (kpos < seq_m), 0.0, NEG_BIG)
        biases.append(bias_ref[...] + kvalid)

    for r in range(d):
        q = rows(q_ref, r, cm)
        kw = jnp.concatenate([rows(kp_ref, r, rad), rows(kc_ref, r, cm), rows(kn_ref, r, rad)], 0)
        vw = jnp.concatenate([rows(vp_ref, r, rad), rows(vc_ref, r, cm), rows(vn_ref, r, rad)], 0)
        for u in range(cm // t):
            m0 = u * t
            s = lax.dot_general(q[m0:m0 + t], kw[m0:m0 + win], (((1,), (1,)), ((), ())),
                                preferred_element_type=F32) + biases[u]
            m = jnp.max(s, axis=-1, keepdims=True)
            p = jnp.exp2(s - m)
            l = jnp.sum(p, axis=-1, keepdims=True)
            o = jnp.dot(p.astype(BF16), vw[m0:m0 + win], preferred_element_type=F32) / l
            lse = jnp.broadcast_to(m + jnp.log2(l), (t, HEAD_DIM))
            if d == 1:
                o_ref[m0:m0 + t, :] = o.astype(o_ref.dtype)
                lse_ref[m0:m0 + t, :] = lse
            else:
                o_scr, lse_scr = scratch
                o_scr[pl.ds(r + d * m0, t, stride=d), :] = o
                lse_scr[pl.ds(r + d * m0, t, stride=d), :] = lse
    if d > 1:
        o_ref[...] = scratch[0][...].astype(o_ref.dtype)
        lse_ref[...] = scratch[1][...]


def _banded_group(q, k, v, g, head0):
    b, _, s, _ = q.shape
    d = DILATIONS[g]
    hpg = HEADS_PER_GROUP_A
    cm = Q_CHUNK // d
    t = min(BAND_TILE, cm)
    halo = BAND_RADIUS * d
    per = Q_CHUNK // halo
    last = s // halo - 1
    blk = pl.BlockSpec((None, None, Q_CHUNK, HEAD_DIM), lambda bi, hg, c: (bi, head0 + hg, c, 0))
    prev = pl.BlockSpec((None, None, halo, HEAD_DIM),
                        lambda bi, hg, c: (bi, head0 + hg, jnp.maximum(c * per - 1, 0), 0))
    nxt = pl.BlockSpec((None, None, halo, HEAD_DIM),
                       lambda bi, hg, c: (bi, head0 + hg, jnp.minimum((c + 1) * per, last), 0))
    out = pl.BlockSpec((None, None, Q_CHUNK, HEAD_DIM), lambda bi, hg, c: (bi, hg, c, 0))
    scratch = [] if d == 1 else [pltpu.VMEM((Q_CHUNK, HEAD_DIM), F32)] * 2
    return pl.pallas_call(
        functools.partial(_band_kernel, d=d, t=t, cm=cm, seq_m=s // d),
        grid=(b, hpg, s // Q_CHUNK),
        in_specs=[blk, prev, blk, nxt, prev, blk, nxt,
                  pl.BlockSpec((t, t + 2 * BAND_RADIUS), lambda bi, hg, c: (0, 0))],
        out_specs=[out, out],
        out_shape=[jax.ShapeDtypeStruct((b, hpg, s, HEAD_DIM), BF16),
                   jax.ShapeDtypeStruct((b, hpg, s, HEAD_DIM), F32)],
        scratch_shapes=scratch,
        compiler_params=_params(("parallel", "parallel", "arbitrary")),
        name=f"banded_attention_d{d}",
    )(q, k, k, k, v, v, v, jnp.asarray(_band_bias(t)))


def _dense_kernel(q_ref, qn_ref, k_ref, vt_ref, o_ref, acc_ref, s_ref, mx_ref, *, tq, tk):
    nq = GQA_GROUP * tq
    n_chunks = k_ref.shape[0] // tk

    def stacked(ref):
        return jnp.concatenate([ref[:, g * HEAD_DIM:(g + 1) * HEAD_DIM]
                                for g in range(GQA_GROUP)], axis=0)

    acc_ref[...] = jnp.zeros(acc_ref.shape, F32)

    def scores(c, slot, ref=q_ref):
        k0 = pl.multiple_of(c * tk, tk)
        st = lax.dot_general(k_ref[pl.ds(k0, tk), :], stacked(ref), (((1,), (1,)), ((), ())),
                             preferred_element_type=F32)
        s_ref[slot] = st
        mx_ref[slot] = jnp.max(st, axis=0, keepdims=True)

    ones_rows = jnp.ones((SUM_ROWS, tk), BF16)

    def update(c, slot, m):
        m_new = jnp.maximum(m, mx_ref[slot])
        alpha = jnp.exp2(m - m_new)
        p = jnp.exp2(s_ref[slot] - m_new).astype(BF16)
        vt1 = jnp.concatenate([vt_ref[c], ones_rows], axis=0)
        pv = jnp.dot(vt1, p, preferred_element_type=F32)
        acc_ref[...] = acc_ref[...] * alpha + pv
        return m_new

    @pl.when(pl.program_id(2) == 0)
    def _():
        scores(0, 0)

    def pair(j, m, last):
        c0 = 2 * j
        scores(c0 + 1, 1)
        m = update(c0, 0, m)
        if last:
            scores(0, 0, qn_ref)
        else:
            scores(c0 + 2, 0)
        return update(c0 + 1, 1, m)

    m0 = jnp.full((1, nq), NEG_BIG, F32)
    n_pairs = n_chunks // 2
    m = lax.fori_loop(0, n_pairs - 1, functools.partial(pair, last=False), m0)
    pair(n_pairs - 1, m, last=True)
    o = acc_ref[0:HEAD_DIM, :] / acc_ref[HEAD_DIM:HEAD_DIM + 1, :]
    for g in range(GQA_GROUP):
        o_ref[:, g * HEAD_DIM:(g + 1) * HEAD_DIM] = o[:, g * tq:(g + 1) * tq].T.astype(o_ref.dtype)


def _dense_attention(qb, kb, vbt, tq=DENSE_TQ):
    b, s, dq = qb.shape
    tk = vbt.shape[-1]
    gw = GQA_GROUP * HEAD_DIM
    n_tiles = s // tq
    return pl.pallas_call(
        functools.partial(_dense_kernel, tq=tq, tk=tk),
        grid=(b, N_KV_B, n_tiles),
        in_specs=[pl.BlockSpec((None, tq, gw), lambda bi, h, i: (bi, i, h)),
                  pl.BlockSpec((None, tq, gw),
                               lambda bi, h, i: (bi, jnp.minimum(i + 1, n_tiles - 1), h)),
                  pl.BlockSpec((None, None, s, HEAD_DIM), lambda bi, h, i: (bi, h, 0, 0)),
                  pl.BlockSpec((None, None, s // tk, HEAD_DIM, tk),
                               lambda bi, h, i: (bi, h, 0, 0, 0))],
        out_specs=pl.BlockSpec((None, tq, gw), lambda bi, h, i: (bi, i, h)),
        out_shape=jax.ShapeDtypeStruct((b, s, dq), BF16),
        scratch_shapes=[pltpu.VMEM((HEAD_DIM + SUM_ROWS, GQA_GROUP * tq), F32),
                        pltpu.VMEM((2, tk, GQA_GROUP * tq), F32),
                        pltpu.VMEM((2, 1, GQA_GROUP * tq), F32)],
        compiler_params=_params(("arbitrary", "arbitrary", "arbitrary")),
        name="dense_gqa_attention",
    )(qb, qb, kb, vbt)


def _merge_kernel(*refs, tm, rc, ng):
    o_refs, lse_refs = refs[0:ng], refs[ng:2 * ng]
    yb_ref, wa_ref, wb_ref, g_ref, out_ref, ya_ref = refs[2 * ng:]
    d = out_ref.shape[-1]

    for h in range(HEADS_PER_GROUP_A):
        lses = [lse_refs[g][h] for g in range(ng)]
        top = functools.reduce(jnp.maximum, lses)
        es = [jnp.exp2(x - top) for x in lses]
        num = sum(e * o_refs[g][h].astype(F32) for g, e in enumerate(es))
        ya_ref[:, h * HEAD_DIM:(h + 1) * HEAD_DIM] = (num / sum(es)).astype(ya_ref.dtype)

    for r in range(tm // rc):
        r0, r1 = r * rc, (r + 1) * rc
        pa = jnp.dot(ya_ref[r0:r1, :], wa_ref[...], preferred_element_type=F32)
        pb = jnp.dot(yb_ref[r0:r1, :], wb_ref[...], preferred_element_type=F32)
        sa = g_ref[r0:r1, 0:d].astype(F32)
        sb = g_ref[r0:r1, d:2 * d].astype(F32)
        out_ref[r0:r1, :] = (sa * pa + sb * pb).astype(out_ref.dtype)


def _gated_merge(group_outs, group_lses, yb, wa, wb, gates, tm=512, rc=256):
    ng = len(group_outs)
    b, hpg, s, _ = group_outs[0].shape
    da = hpg * HEAD_DIM
    db = yb.shape[-1]
    d = wa.shape[1]
    hblk = pl.BlockSpec((None, hpg, tm, HEAD_DIM), lambda bi, i: (bi, 0, i, 0))
    once = pl.Buffered(1)
    return pl.pallas_call(
        functools.partial(_merge_kernel, tm=tm, rc=rc, ng=ng),
        grid=(b, s // tm),
        in_specs=[hblk] * (2 * ng) + [
            pl.BlockSpec((None, tm, db), lambda bi, i: (bi, i, 0)),
            pl.BlockSpec((da, d), lambda bi, i: (0, 0), pipeline_mode=once),
            pl.BlockSpec((db, d), lambda bi, i: (0, 0), pipeline_mode=once),
            pl.BlockSpec((None, tm, 2 * d), lambda bi, i: (bi, i, 0))],
        out_specs=pl.BlockSpec((None, tm, d), lambda bi, i: (bi, i, 0)),
        out_shape=jax.ShapeDtypeStruct((b, s, d), BF16),
        scratch_shapes=[pltpu.VMEM((tm, da), BF16)],
        compiler_params=_params(("parallel", "parallel")),
        name="gated_merge",
    )(*group_outs, *group_lses, yb, wa, wb, gates)


def _oproj_kernel(m_ref, w_ref, x_ref, o_ref, *, tm, rc):
    for r in range(tm // rc):
        r0, r1 = r * rc, (r + 1) * rc
        o_ref[r0:r1, :] = x_ref[r0:r1, :] + jnp.dot(m_ref[r0:r1, :], w_ref[...],
                                                    preferred_element_type=F32)


def _oproj_residual(merged, w_o, x, tm=1024, tn=1024, rc=256):
    b, s, d = merged.shape
    return pl.pallas_call(
        functools.partial(_oproj_kernel, tm=tm, rc=rc),
        grid=(b, s // tm, d // tn),
        in_specs=[pl.BlockSpec((None, tm, d), lambda bi, i, j: (bi, i, 0)),
                  pl.BlockSpec((d, tn), lambda bi, i, j: (0, j)),
                  pl.BlockSpec((None, tm, tn), lambda bi, i, j: (bi, i, j))],
        out_specs=pl.BlockSpec((None, tm, tn), lambda bi, i, j: (bi, i, j)),
        out_shape=jax.ShapeDtypeStruct((b, s, d), F32),
        compiler_params=_params(("parallel", "parallel", "arbitrary")),
        name="oproj_residual",
    )(merged, w_o, x)


def _ffn_kernel(x_ref, gffn_ref, gfin_ref, wg_ref, wu_ref, wd_ref, o_ref, h_ref, *, tm, rc):
    f = pl.program_id(2)

    @pl.when(f == 0)
    def _():
        x = x_ref[...]
        ms = jnp.mean(x * x, axis=-1, keepdims=True)
        h_ref[...] = (x * lax.rsqrt(ms + EPS) * gffn_ref[...]).astype(h_ref.dtype)
        o_ref[...] = jnp.zeros(o_ref.shape, F32)

    for r in range(tm // rc):
        r0, r1 = r * rc, (r + 1) * rc
        h = h_ref[r0:r1, :]
        gate = jnp.dot(h, wg_ref[...], preferred_element_type=F32)
        up = jnp.dot(h, wu_ref[...], preferred_element_type=F32)
        act = (gate / (1.0 + jnp.exp(-gate))) * up
        o_ref[r0:r1, :] += jnp.dot(act.astype(BF16), wd_ref[...], preferred_element_type=F32)

    @pl.when(f == pl.num_programs(2) - 1)
    def _():
        y = x_ref[...] + o_ref[...]
        ms = jnp.mean(y * y, axis=-1, keepdims=True)
        o_ref[...] = y * lax.rsqrt(ms + EPS) * gfin_ref[...]


def _ffn_final(x, g_ffn, g_final, w_gu, w_down, tm=1024, tf=512, rc=512):
    b, s, d = x.shape
    dff = w_down.shape[0]
    nf = dff // tf
    return pl.pallas_call(
        functools.partial(_ffn_kernel, tm=tm, rc=rc),
        grid=(b, s // tm, nf),
        in_specs=[pl.BlockSpec((None, tm, d), lambda bi, i, f: (bi, i, 0)),
                  pl.BlockSpec((1, d), lambda bi, i, f: (0, 0)),
                  pl.BlockSpec((1, d), lambda bi, i, f: (0, 0)),
                  pl.BlockSpec((d, tf), lambda bi, i, f: (0, f)),
                  pl.BlockSpec((d, tf), lambda bi, i, f: (0, nf + f)),
                  pl.BlockSpec((tf, d), lambda bi, i, f: (f, 0))],
        out_specs=pl.BlockSpec((None, tm, d), lambda bi, i, f: (bi, i, 0)),
        out_shape=jax.ShapeDtypeStruct((b, s, d), F32),
        scratch_shapes=[pltpu.VMEM((tm, d), BF16)],
        compiler_params=_params(("parallel", "parallel", "arbitrary"), vmem_limit=FFN_VMEM_LIMIT),
        name="ffn_final",
    )(x, g_ffn.reshape(1, d).astype(F32), g_final.reshape(1, d).astype(F32), w_gu, w_gu, w_down)


def _rope_tables(s):
    half = HEAD_DIM // 2
    inv = ROPE_THETA ** (-(jnp.arange(half, dtype=F32) * 2.0 / HEAD_DIM))
    ang = jnp.arange(s).astype(F32)[:, None] * inv[None, :]
    cos, sin = jnp.cos(ang), jnp.sin(ang)
    return jnp.concatenate([cos, cos], axis=-1), jnp.concatenate([-sin, sin], axis=-1)


def _axial_tables(s):
    dim = HEAD_DIM // 2
    half = dim // 2
    inv = ROPE_THETA ** (-(jnp.arange(half, dtype=F32) * 2.0 / dim))
    n_rows = s // GRID_W
    row = jnp.repeat(jnp.arange(n_rows), GRID_W).astype(F32)
    col = jnp.tile(jnp.arange(GRID_W), n_rows).astype(F32)
    cs, sn = [], []
    for pos in (row, col):
        ang = pos[:, None] * inv[None, :]
        c, sgn = jnp.cos(ang), jnp.sin(ang)
        cs += [c, c]
        sn += [-sgn, sgn]
    return jnp.concatenate(cs, axis=-1), jnp.concatenate(sn, axis=-1)


def _trunk(x, p):
    b, s, d = x.shape
    a_w = N_HEADS_A * HEAD_DIM
    b_q = d
    b_kv = N_KV_B * HEAD_DIM
    cuts = np.cumsum([0, a_w, a_w, a_w, b_q, b_kv, b_kv, 2 * d]).tolist()
    w_in = p["w_in"]

    def wcols(k):
        return w_in, cuts[k], cuts[k + 1] - cuts[k]

    h = _rmsnorm_bf16(x, p["g_attn"])
    cos1, sin1 = p["rope"]
    cos2, sin2 = p["axial"]

    gw = HEADS_PER_GROUP_A * HEAD_DIM
    qkv_a = []
    for col0, n, dt, tag in ((0, gw, BF16, "g0"), (gw, a_w - gw, F32, "g12")):
        qkv_a.append((
            _proj(h, w_in, cuts[0] + col0, n, kind="rope", out_kind="heads", out_dtype=dt,
                  tables=(cos1 * QSCALE, sin1 * QSCALE), name="proj_qa_" + tag),
            _proj(h, w_in, cuts[1] + col0, n, kind="rope", out_kind="heads", out_dtype=dt,
                  tables=(cos1, sin1), name="proj_ka_" + tag),
            _proj(h, w_in, cuts[2] + col0, n, kind="plain", out_kind="heads", out_dtype=dt,
                  name="proj_va_" + tag)))
    qb = _proj(h, *wcols(3), kind="qknorm", out_kind="natural", tables=(cos2, sin2),
               gain=p["q_gain"] * QSCALE, name="proj_qb")
    kb = _proj(h, *wcols(4), kind="qknorm", out_kind="heads", tables=(cos2, sin2),
               gain=p["k_gain"], name="proj_kb")
    vbt = _proj(h, *wcols(5), kind="plain", out_kind="vT", rc=DENSE_TK, name="proj_vb")
    gates = _proj(h, *wcols(6), kind="sigmoid", out_kind="natural", name="proj_gates")

    groups = [_banded_group(*qkv_a[0], 0, 0)]
    groups += [_banded_group(*qkv_a[1], g, (g - 1) * HEADS_PER_GROUP_A)
               for g in range(1, len(DILATIONS))]
    yb = _dense_attention(qb, kb, vbt)
    merged = _gated_merge([o for o, _ in groups], [l for _, l in groups], yb,
                          p["w_a_br"], p["w_b_br"], gates)
    x1 = _oproj_residual(merged, p["w_o"], x)
    return _ffn_final(x1, p["g_ffn"], p["g_final"], p["w_gate_up"], p["w_down"])


def kernel(x_prompt, x_sample, g_attn, w_in, q_gain_b, k_gain_b, w_a_br, w_b_br, w_o, g_ffn,
           w_gate_up, w_down, g_final):
    assert g_attn.shape[0] == 1, "single-layer trunk"
    s = x_prompt.shape[1]
    assert x_sample.shape[1] == s and s % Q_CHUNK == 0
    p = {
        "g_attn": g_attn[0], "g_ffn": g_ffn[0], "g_final": g_final,
        "w_in": w_in[0].astype(BF16),
        "q_gain": q_gain_b[0].reshape(1, HEAD_DIM).astype(F32),
        "k_gain": k_gain_b[0].reshape(1, HEAD_DIM).astype(F32),
        "w_a_br": w_a_br[0].astype(BF16), "w_b_br": w_b_br[0].astype(BF16),
        "w_o": w_o[0].astype(BF16),
        "w_gate_up": w_gate_up[0].astype(BF16), "w_down": w_down[0].astype(BF16),
        "rope": _rope_tables(s), "axial": _axial_tables(s),
    }
    return (_trunk(x_prompt, p), _trunk(x_sample, p))
```

```python
import functools
import math

import numpy as np
import jax
import jax.numpy as jnp
from jax import lax
from jax.experimental import pallas as pl
from jax.experimental.pallas import tpu as pltpu

HEAD_DIM = 128
DILATIONS = (1, 4, 16)
BAND_RADIUS = 64
HEADS_PER_GROUP_A = 8
N_HEADS_A = HEADS_PER_GROUP_A * len(DILATIONS)
N_KV_B = 4
GQA_GROUP = 4
GRID_W = 64
ROPE_THETA = 10000.0
EPS = 1e-6
NEG_BIG = -1e30

LANES = 128
VMEM_BYTES_V7X = 64 * 1024 * 1024
VMEM_LIMIT = VMEM_BYTES_V7X - 8 * 1024 * 1024
FFN_VMEM_LIMIT = VMEM_BYTES_V7X - 3 * 1024 * 1024

F32 = jnp.float32
BF16 = jnp.bfloat16

QSCALE = (HEAD_DIM ** -0.5) * math.log2(math.e)

KV_PAD = BAND_RADIUS * max(DILATIONS)
Q_CHUNK = 2 * KV_PAD
BAND_TILE = 256

DENSE_TQ = 256
DENSE_TK = 1024
SUM_ROWS = 16


def _params(sem, vmem_limit=VMEM_LIMIT, flags=None):
    return pltpu.CompilerParams(dimension_semantics=sem, vmem_limit_bytes=vmem_limit, flags=flags)


def _rmsnorm_kernel(x_ref, g_ref, o_ref):
    x = x_ref[...]
    ms = jnp.mean(x * x, axis=-1, keepdims=True)
    o_ref[...] = (x * lax.rsqrt(ms + EPS) * g_ref[...]).astype(o_ref.dtype)


def _rmsnorm_bf16(x, g, tm=512):
    b, s, d = x.shape
    return pl.pallas_call(
        _rmsnorm_kernel,
        grid=(b, s // tm),
        in_specs=[pl.BlockSpec((None, tm, d), lambda bi, i: (bi, i, 0)),
                  pl.BlockSpec((1, d), lambda bi, i: (0, 0))],
        out_specs=pl.BlockSpec((None, tm, d), lambda bi, i: (bi, i, 0)),
        out_shape=jax.ShapeDtypeStruct((b, s, d), BF16),
        compiler_params=_params(("parallel", "parallel")),
        name="rmsnorm_bf16",
    )(x, g.reshape(1, d).astype(F32))


def _proj_kernel(*refs, kind, out_kind, nh, rc, tm, pad_tiles, src_tiles):
    h_ref, w_ref = refs[0], refs[1]
    o_ref = refs[-1]
    if kind in ("rope", "qknorm"):
        cos_ref, sin_ref = refs[2], refs[3]
    if kind == "qknorm":
        gain_ref = refs[4]
        lane = lax.broadcasted_iota(jnp.int32, (rc, HEAD_DIM), 1)
        lo = (lane % 64) < 32

    def epilogue(x, c, s):
        if kind == "rope":
            return x * c + pltpu.roll(x, 64, 1) * s
        if kind == "qknorm":
            ms = jnp.mean(x * x, axis=-1, keepdims=True)
            y = x * lax.rsqrt(ms + EPS) * gain_ref[...]
            yr = jnp.where(lo, pltpu.roll(y, 96, 1), pltpu.roll(y, 32, 1))
            return y * c + yr * s
        if kind == "sigmoid":
            return 1.0 / (1.0 + jnp.exp(-x))
        return x

    def compute():
        for r in range(tm // rc):
            r0, r1 = r * rc, (r + 1) * rc
            acc = jnp.dot(h_ref[r0:r1, :], w_ref[...], preferred_element_type=F32)
            c = s = None
            if kind in ("rope", "qknorm"):
                c, s = cos_ref[r0:r1, :], sin_ref[r0:r1, :]
            for j in range(nh):
                y = epilogue(acc[:, j * HEAD_DIM:(j + 1) * HEAD_DIM], c, s)
                if out_kind == "heads":
                    o_ref[j, r0:r1, :] = y.astype(o_ref.dtype)
                elif out_kind == "natural":
                    o_ref[r0:r1, j * HEAD_DIM:(j + 1) * HEAD_DIM] = y.astype(o_ref.dtype)
                else:
                    o_ref[j, r] = y.T.astype(o_ref.dtype)

    if pad_tiles == 0:
        compute()
    else:
        i = pl.program_id(1)
        is_pad = jnp.logical_or(i < pad_tiles, i >= pad_tiles + src_tiles)

        @pl.when(is_pad)
        def _():
            o_ref[...] = jnp.zeros(o_ref.shape, o_ref.dtype)

        @pl.when(jnp.logical_not(is_pad))
        def _():
            compute()


def _proj(h, w, col0, n, *, kind, out_kind, out_dtype=BF16, tables=(), gain=None, pad=0,
          tm=1024, tn=1024, rc=256, name):
    b, s, d = h.shape
    tn = min(tn, n)
    assert col0 % tn == 0 and n % tn == 0
    cb0 = col0 // tn
    nh = tn // HEAD_DIM
    src_tiles = s // tm
    pad_tiles = pad // tm
    rows = s + 2 * pad

    def src(i):
        if pad_tiles == 0:
            return i
        return jnp.clip(i - pad_tiles, 0, src_tiles - 1)

    in_specs = [pl.BlockSpec((None, tm, d), lambda bi, i, j: (bi, src(i), 0)),
                pl.BlockSpec((d, tn), lambda bi, i, j: (0, cb0 + j))]
    args = [h, w]
    for t in tables:
        in_specs.append(pl.BlockSpec((tm, HEAD_DIM), lambda bi, i, j: (src(i), 0)))
        args.append(t)
    if gain is not None:
        in_specs.append(pl.BlockSpec((1, HEAD_DIM), lambda bi, i, j: (0, 0)))
        args.append(gain)

    if out_kind == "heads":
        out_shape = (b, n // HEAD_DIM, rows, HEAD_DIM)
        out_spec = pl.BlockSpec((None, nh, tm, HEAD_DIM), lambda bi, i, j: (bi, j, i, 0))
    elif out_kind == "natural":
        out_shape = (b, rows, n)
        out_spec = pl.BlockSpec((None, tm, tn), lambda bi, i, j: (bi, i, j))
    else:
        out_shape = (b, n // HEAD_DIM, rows // rc, HEAD_DIM, rc)
        out_spec = pl.BlockSpec((None, nh, tm // rc, HEAD_DIM, rc),
                                lambda bi, i, j: (bi, j, i, 0, 0))

    kern = functools.partial(_proj_kernel, kind=kind, out_kind=out_kind, nh=nh, rc=rc, tm=tm,
                             pad_tiles=pad_tiles, src_tiles=src_tiles)
    return pl.pallas_call(
        kern,
        grid=(b, rows // tm, n // tn),
        in_specs=in_specs,
        out_specs=out_spec,
        out_shape=jax.ShapeDtypeStruct(out_shape, out_dtype),
        compiler_params=_params(("parallel", "parallel", "arbitrary")),
        name=name,
    )(*args)


def _band_bias(t):
    qi = np.arange(t)[:, None]
    kj = np.arange(t + 2 * BAND_RADIUS)[None, :]
    ok = np.abs(kj - BAND_RADIUS - qi) <= BAND_RADIUS
    return np.where(ok, 0.0, NEG_BIG).astype(np.float32)


def _band_kernel(q_ref, kp_ref, kc_ref, kn_ref, vp_ref, vc_ref, vn_ref, bias_ref,
                 o_ref, lse_ref, *scratch, d, t, cm, seq_m):
    c = pl.program_id(2)
    rad = BAND_RADIUS
    win = t + 2 * rad

    def rows(ref, r, n):
        if d == 1:
            return ref[...]
        return ref[pl.ds(r, n, stride=d), :].astype(BF16)

    biases = []
    for u in range(cm // t):
        kpos = c * cm + u * t - rad + lax.broadcasted_iota(jnp.int32, (1, win), 1)
        kvalid = jnp.where((kpos >= 0) & (kpos < seq_m), 0.0, NEG_BIG)
        biases.append(bias_ref[...] + kvalid)

    for r in range(d):
        q = rows(q_ref, r, cm)
        kw = jnp.concatenate([rows(kp_ref, r, rad), rows(kc_ref, r, cm), rows(kn_ref, r, rad)], 0)
        vw = jnp.concatenate([rows(vp_ref, r, rad), rows(vc_ref, r, cm), rows(vn_ref, r, rad)], 0)
        for u in range(cm // t):
            m0 = u * t
            s = lax.dot_general(q[m0:m0 + t], kw[m0:m0 + win], (((1,), (1,)), ((), ())),
                                preferred_element_type=F32) + biases[u]
            m = jnp.max(s, axis=-1, keepdims=True)
            p = jnp.exp2(s - m)
            l = jnp.sum(p, axis=-1, keepdims=True)
            o = jnp.dot(p.astype(BF16), vw[m0:m0 + win], preferred_element_type=F32) / l
            lse = jnp.broadcast_to(m + jnp.log2(l), (t, HEAD_DIM))
            if d == 1:
                o_ref[m0:m0 + t, :] = o.astype(o_ref.dtype)
                lse_ref[m0:m0 + t, :] = lse
            else:
                o_scr, lse_scr = scratch
                o_scr[pl.ds(r + d * m0, t, stride=d), :] = o
                lse_scr[pl.ds(r + d * m0, t, stride=d), :] = lse
    if d > 1:
        o_ref[...] = scratch[0][...].astype(o_ref.dtype)
        lse_ref[...] = scratch[1][...]


def _banded_group(q, k, v, g, head0):
    b, _, s, _ = q.shape
    d = DILATIONS[g]
    hpg = HEADS_PER_GROUP_A
    cm = Q_CHUNK // d
    t = min(BAND_TILE, cm)
    halo = BAND_RADIUS * d
    per = Q_CHUNK // halo
    last = s // halo - 1
    blk = pl.BlockSpec((None, None, Q_CHUNK, HEAD_DIM), lambda bi, hg, c: (bi, head0 + hg, c, 0))
    prev = pl.BlockSpec((None, None, halo, HEAD_DIM),
                        lambda bi, hg, c: (bi, head0 + hg, jnp.maximum(c * per - 1, 0), 0))
    nxt = pl.BlockSpec((None, None, halo, HEAD_DIM),
                       lambda bi, hg, c: (bi, head0 + hg, jnp.minimum((c + 1) * per, last), 0))
    out = pl.BlockSpec((None, None, Q_CHUNK, HEAD_DIM), lambda bi, hg, c: (bi, hg, c, 0))
    scratch = [] if d == 1 else [pltpu.VMEM((Q_CHUNK, HEAD_DIM), F32)] * 2
    return pl.pallas_call(
        functools.partial(_band_kernel, d=d, t=t, cm=cm, seq_m=s // d),
        grid=(b, hpg, s // Q_CHUNK),
        in_specs=[blk, prev, blk, nxt, prev, blk, nxt,
                  pl.BlockSpec((t, t + 2 * BAND_RADIUS), lambda bi, hg, c: (0, 0))],
        out_specs=[out, out],
        out_shape=[jax.ShapeDtypeStruct((b, hpg, s, HEAD_DIM), BF16),
                   jax.ShapeDtypeStruct((b, hpg, s, HEAD_DIM), F32)],
        scratch_shapes=scratch,
        compiler_params=_params(("parallel", "parallel", "arbitrary")),
        name=f"banded_attention_d{d}",
    )(q, k, k, k, v, v, v, jnp.asarray(_band_bias(t)))


def _dense_kernel(q_ref, qn_ref, k_ref, vt_ref, o_ref, acc_ref, s_ref, mx_ref, *, tq, tk):
    n_chunks = k_ref.shape[0] // tk
    heads = range(GQA_GROUP)

    def strip(g):
        return slice(g * tq, (g + 1) * tq)

    acc_ref[...] = jnp.zeros(acc_ref.shape, F32)

    def scores(c, slot, g, ref=q_ref):
        k0 = pl.multiple_of(c * tk, tk)
        q = ref[:, g * HEAD_DIM:(g + 1) * HEAD_DIM]
        st = lax.dot_general(k_ref[pl.ds(k0, tk), :], q, (((1,), (1,)), ((), ())),
                             preferred_element_type=F32)
        s_ref[slot, :, strip(g)] = st
        mx_ref[slot, :, strip(g)] = jnp.max(st, axis=0, keepdims=True)

    ones_rows = jnp.ones((SUM_ROWS, tk), BF16)

    def update(c, slot, g, m):
        m_new = jnp.maximum(m, mx_ref[slot, :, strip(g)])
        alpha = jnp.exp2(m - m_new)
        p = jnp.exp2(s_ref[slot, :, strip(g)] - m_new).astype(BF16)
        vt1 = jnp.concatenate([vt_ref[c], ones_rows], axis=0)
        pv = jnp.dot(vt1, p, preferred_element_type=F32)
        acc_ref[:, strip(g)] = acc_ref[:, strip(g)] * alpha + pv
        return m_new

    @pl.when(pl.program_id(2) == 0)
    def _():
        for g in heads:
            scores(0, 0, g)

    def half(c_next, next_ref, c, slot, ms):
        out = []
        for g in heads:
            scores(c_next, 1 - slot, g, next_ref)
            out.append(update(c, slot, g, ms[g]))
        return tuple(out)

    def pair(j, ms, last):
        c0 = 2 * j
        ms = half(c0 + 1, q_ref, c0, 0, ms)
        if last:
            return half(0, qn_ref, c0 + 1, 1, ms)
        return half(c0 + 2, q_ref, c0 + 1, 1, ms)

    m0 = tuple(jnp.full((1, tq), NEG_BIG, F32) for _ in heads)
    n_pairs = n_chunks // 2
    ms = lax.fori_loop(0, n_pairs - 1, functools.partial(pair, last=False), m0)
    pair(n_pairs - 1, ms, last=True)
    for g in heads:
        o = acc_ref[0:HEAD_DIM, strip(g)] / acc_ref[HEAD_DIM:HEAD_DIM + 1, strip(g)]
        o_ref[:, g * HEAD_DIM:(g + 1) * HEAD_DIM] = o.T.astype(o_ref.dtype)


def _dense_attention(qb, kb, vbt, tq=DENSE_TQ):
    b, s, dq = qb.shape
    tk = vbt.shape[-1]
    gw = GQA_GROUP * HEAD_DIM
    n_tiles = s // tq
    return pl.pallas_call(
        functools.partial(_dense_kernel, tq=tq, tk=tk),
        grid=(b, N_KV_B, n_tiles),
        in_specs=[pl.BlockSpec((None, tq, gw), lambda bi, h, i: (bi, i, h)),
                  pl.BlockSpec((None, tq, gw),
                               lambda bi, h, i: (bi, jnp.minimum(i + 1, n_tiles - 1), h)),
                  pl.BlockSpec((None, None, s, HEAD_DIM), lambda bi, h, i: (bi, h, 0, 0)),
                  pl.BlockSpec((None, None, s // tk, HEAD_DIM, tk),
                               lambda bi, h, i: (bi, h, 0, 0, 0))],
        out_specs=pl.BlockSpec((None, tq, gw), lambda bi, h, i: (bi, i, h)),
        out_shape=jax.ShapeDtypeStruct((b, s, dq), BF16),
        scratch_shapes=[pltpu.VMEM((HEAD_DIM + SUM_ROWS, GQA_GROUP * tq), F32),
                        pltpu.VMEM((2, tk, GQA_GROUP * tq), F32),
                        pltpu.VMEM((2, 1, GQA_GROUP * tq), F32)],
        compiler_params=_params(("arbitrary", "arbitrary", "arbitrary")),
        name="dense_gqa_attention",
    )(qb, qb, kb, vbt)


def _merge_kernel(*refs, tm, rc, ng):
    o_refs, lse_refs = refs[0:ng], refs[ng:2 * ng]
    yb_ref, wa_ref, wb_ref, g_ref, out_ref, ya_ref = refs[2 * ng:]
    d = out_ref.shape[-1]

    for h in range(HEADS_PER_GROUP_A):
        lses = [lse_refs[g][h] for g in range(ng)]
        top = functools.reduce(jnp.maximum, lses)
        es = [jnp.exp2(x - top) for x in lses]
        num = sum(e * o_refs[g][h].astype(F32) for g, e in enumerate(es))
        ya_ref[:, h * HEAD_DIM:(h + 1) * HEAD_DIM] = (num / sum(es)).astype(ya_ref.dtype)

    for r in range(tm // rc):
        r0, r1 = r * rc, (r + 1) * rc
        pa = jnp.dot(ya_ref[r0:r1, :], wa_ref[...], preferred_element_type=F32)
        pb = jnp.dot(yb_ref[r0:r1, :], wb_ref[...], preferred_element_type=F32)
        sa = g_ref[r0:r1, 0:d].astype(F32)
        sb = g_ref[r0:r1, d:2 * d].astype(F32)
        out_ref[r0:r1, :] = (sa * pa + sb * pb).astype(out_ref.dtype)


def _gated_merge(group_outs, group_lses, yb, wa, wb, gates, tm=512, rc=256):
    ng = len(group_outs)
    b, hpg, s, _ = group_outs[0].shape
    da = hpg * HEAD_DIM
    db = yb.shape[-1]
    d = wa.shape[1]
    hblk = pl.BlockSpec((None, hpg, tm, HEAD_DIM), lambda bi, i: (bi, 0, i, 0))
    once = pl.Buffered(1)
    return pl.pallas_call(
        functools.partial(_merge_kernel, tm=tm, rc=rc, ng=ng),
        grid=(b, s // tm),
        in_specs=[hblk] * (2 * ng) + [
            pl.BlockSpec((None, tm, db), lambda bi, i: (bi, i, 0)),
            pl.BlockSpec((da, d), lambda bi, i: (0, 0), pipeline_mode=once),
            pl.BlockSpec((db, d), lambda bi, i: (0, 0), pipeline_mode=once),
            pl.BlockSpec((None, tm, 2 * d), lambda bi, i: (bi, i, 0))],
        out_specs=pl.BlockSpec((None, tm, d), lambda bi, i: (bi, i, 0)),
        out_shape=jax.ShapeDtypeStruct((b, s, d), BF16),
        scratch_shapes=[pltpu.VMEM((tm, da), BF16)],
        compiler_params=_params(("parallel", "parallel")),
        name="gated_merge",
    )(*group_outs, *group_lses, yb, wa, wb, gates)


def _oproj_kernel(m_ref, w_ref, x_ref, o_ref, *, tm, rc):
    for r in range(tm // rc):
        r0, r1 = r * rc, (r + 1) * rc
        o_ref[r0:r1, :] = x_ref[r0:r1, :] + jnp.dot(m_ref[r0:r1, :], w_ref[...],
                                                    preferred_element_type=F32)


def _oproj_residual(merged, w_o, x, tm=1024, tn=1024, rc=256):
    b, s, d = merged.shape
    return pl.pallas_call(
        functools.partial(_oproj_kernel, tm=tm, rc=rc),
        grid=(b, s // tm, d // tn),
        in_specs=[pl.BlockSpec((None, tm, d), lambda bi, i, j: (bi, i, 0)),
                  pl.BlockSpec((d, tn), lambda bi, i, j: (0, j)),
                  pl.BlockSpec((None, tm, tn), lambda bi, i, j: (bi, i, j))],
        out_specs=pl.BlockSpec((None, tm, tn), lambda bi, i, j: (bi, i, j)),
        out_shape=jax.ShapeDtypeStruct((b, s, d), F32),
        compiler_params=_params(("parallel", "parallel", "arbitrary")),
        name="oproj_residual",
    )(merged, w_o, x)


def _ffn_kernel(x_ref, gffn_ref, gfin_ref, wg_ref, wu_ref, wd_ref, o_ref, h_ref, *, tm, rc):
    f = pl.program_id(2)

    @pl.when(f == 0)
    def _():
        x = x_ref[...]
        ms = jnp.mean(x * x, axis=-1, keepdims=True)
        h_ref[...] = (x * lax.rsqrt(ms + EPS) * gffn_ref[...]).astype(h_ref.dtype)
        o_ref[...] = jnp.zeros(o_ref.shape, F32)

    for r in range(tm // rc):
        r0, r1 = r * rc, (r + 1) * rc
        h = h_ref[r0:r1, :]
        gate = jnp.dot(h, wg_ref[...], preferred_element_type=F32)
        up = jnp.dot(h, wu_ref[...], preferred_element_type=F32)
        act = (gate / (1.0 + jnp.exp(-gate))) * up
        o_ref[r0:r1, :] += jnp.dot(act.astype(BF16), wd_ref[...], preferred_element_type=F32)

    @pl.when(f == pl.num_programs(2) - 1)
    def _():
        y = x_ref[...] + o_ref[...]
        ms = jnp.mean(y * y, axis=-1, keepdims=True)
        o_ref[...] = y * lax.rsqrt(ms + EPS) * gfin_ref[...]


def _ffn_final(x, g_ffn, g_final, w_gu, w_down, tm=1024, tf=512, rc=512):
    b, s, d = x.shape
    dff = w_down.shape[0]
    nf = dff // tf
    return pl.pallas_call(
        functools.partial(_ffn_kernel, tm=tm, rc=rc),
        grid=(b, s // tm, nf),
        in_specs=[pl.BlockSpec((None, tm, d), lambda bi, i, f: (bi, i, 0)),
                  pl.BlockSpec((1, d), lambda bi, i, f: (0, 0)),
                  pl.BlockSpec((1, d), lambda bi, i, f: (0, 0)),
                  pl.BlockSpec((d, tf), lambda bi, i, f: (0, f)),
                  pl.BlockSpec((d, tf), lambda bi, i, f: (0, nf + f)),
                  pl.BlockSpec((tf, d), lambda bi, i, f: (f, 0))],
        out_specs=pl.BlockSpec((None, tm, d), lambda bi, i, f: (bi, i, 0)),
        out_shape=jax.ShapeDtypeStruct((b, s, d), F32),
        scratch_shapes=[pltpu.VMEM((tm, d), BF16)],
        compiler_params=_params(("parallel", "parallel", "arbitrary"), vmem_limit=FFN_VMEM_LIMIT),
        name="ffn_final",
    )(x, g_ffn.reshape(1, d).astype(F32), g_final.reshape(1, d).astype(F32), w_gu, w_gu, w_down)


def _rope_tables(s):
    half = HEAD_DIM // 2
    inv = ROPE_THETA ** (-(jnp.arange(half, dtype=F32) * 2.0 / HEAD_DIM))
    ang = jnp.arange(s).astype(F32)[:, None] * inv[None, :]
    cos, sin = jnp.cos(ang), jnp.sin(ang)
    return jnp.concatenate([cos, cos], axis=-1), jnp.concatenate([-sin, sin], axis=-1)


def _axial_tables(s):
    dim = HEAD_DIM // 2
    half = dim // 2
    inv = ROPE_THETA ** (-(jnp.arange(half, dtype=F32) * 2.0 / dim))
    n_rows = s // GRID_W
    row = jnp.repeat(jnp.arange(n_rows), GRID_W).astype(F32)
    col = jnp.tile(jnp.arange(GRID_W), n_rows).astype(F32)
    cs, sn = [], []
    for pos in (row, col):
        ang = pos[:, None] * inv[None, :]
        c, sgn = jnp.cos(ang), jnp.sin(ang)
        cs += [c, c]
        sn += [-sgn, sgn]
    return jnp.concatenate(cs, axis=-1), jnp.concatenate(sn, axis=-1)


def _trunk(x, p):
    b, s, d = x.shape
    a_w = N_HEADS_A * HEAD_DIM
    b_q = d
    b_kv = N_KV_B * HEAD_DIM
    cuts = np.cumsum([0, a_w, a_w, a_w, b_q, b_kv, b_kv, 2 * d]).tolist()
    w_in = p["w_in"]

    def wcols(k):
        return w_in, cuts[k], cuts[k + 1] - cuts[k]

    h = _rmsnorm_bf16(x, p["g_attn"])
    cos1, sin1 = p["rope"]
    cos2, sin2 = p["axial"]

    gw = HEADS_PER_GROUP_A * HEAD_DIM
    qkv_a = []
    for col0, n, dt, tag in ((0, gw, BF16, "g0"), (gw, a_w - gw, F32, "g12")):
        qkv_a.append((
            _proj(h, w_in, cuts[0] + col0, n, kind="rope", out_kind="heads", out_dtype=dt,
                  tables=(cos1 * QSCALE, sin1 * QSCALE), name="proj_qa_" + tag),
            _proj(h, w_in, cuts[1] + col0, n, kind="rope", out_kind="heads", out_dtype=dt,
                  tables=(cos1, sin1), name="proj_ka_" + tag),
            _proj(h, w_in, cuts[2] + col0, n, kind="plain", out_kind="heads", out_dtype=dt,
                  name="proj_va_" + tag)))
    qb = _proj(h, *wcols(3), kind="qknorm", out_kind="natural", tables=(cos2, sin2),
               gain=p["q_gain"] * QSCALE, name="proj_qb")
    kb = _proj(h, *wcols(4), kind="qknorm", out_kind="heads", tables=(cos2, sin2),
               gain=p["k_gain"], name="proj_kb")
    vbt = _proj(h, *wcols(5), kind="plain", out_kind="vT", rc=DENSE_TK, name="proj_vb")
    gates = _proj(h, *wcols(6), kind="sigmoid", out_kind="natural", name="proj_gates")

    groups = [_banded_group(*qkv_a[0], 0, 0)]
    groups += [_banded_group(*qkv_a[1], g, (g - 1) * HEADS_PER_GROUP_A)
               for g in range(1, len(DILATIONS))]
    yb = _dense_attention(qb, kb, vbt)
    merged = _gated_merge([o for o, _ in groups], [l for _, l in groups], yb,
                          p["w_a_br"], p["w_b_br"], gates)
    x1 = _oproj_residual(merged, p["w_o"], x)
    return _ffn_final(x1, p["g_ffn"], p["g_final"], p["w_gate_up"], p["w_down"])


def kernel(x_prompt, x_sample, g_attn, w_in, q_gain_b, k_gain_b, w_a_br, w_b_br, w_o, g_ffn,
           w_gate_up, w_down, g_final):
    assert g_attn.shape[0] == 1, "single-layer trunk"
    s = x_prompt.shape[1]
    assert x_sample.shape[1] == s and s % Q_CHUNK == 0
    p = {
        "g_attn": g_attn[0], "g_ffn": g_ffn[0], "g_final": g_final,
        "w_in": w_in[0].astype(BF16),
        "q_gain": q_gain_b[0].reshape(1, HEAD_DIM).astype(F32),
        "k_gain": k_gain_b[0].reshape(1, HEAD_DIM).astype(F32),
        "w_a_br": w_a_br[0].astype(BF16), "w_b_br": w_b_br[0].astype(BF16),
        "w_o": w_o[0].astype(BF16),
        "w_gate_up": w_gate_up[0].astype(BF16), "w_down": w_down[0].astype(BF16),
        "rope": _rope_tables(s), "axial": _axial_tables(s),
    }
    return (_trunk(x_prompt, p), _trunk(x_sample, p))
```

```python
import functools
import math

import numpy as np
import jax
import jax.numpy as jnp
from jax import lax
from jax.experimental import pallas as pl
from jax.experimental.pallas import tpu as pltpu

HEAD_DIM = 128
DILATIONS = (1, 4, 16)
BAND_RADIUS = 64
HEADS_PER_GROUP_A = 8
N_HEADS_A = HEADS_PER_GROUP_A * len(DILATIONS)
N_KV_B = 4
GQA_GROUP = 4
GRID_W = 64
ROPE_THETA = 10000.0
EPS = 1e-6
NEG_BIG = -1e30

LANES = 128
VMEM_BYTES_V7X = 64 * 1024 * 1024
VMEM_LIMIT = VMEM_BYTES_V7X - 8 * 1024 * 1024
FFN_VMEM_LIMIT = VMEM_BYTES_V7X - 3 * 1024 * 1024

F32 = jnp.float32
BF16 = jnp.bfloat16

QSCALE = (HEAD_DIM ** -0.5) * math.log2(math.e)

KV_PAD = BAND_RADIUS * max(DILATIONS)
Q_CHUNK = 2 * KV_PAD
BAND_TILE = 256

DENSE_TQ = 256
DENSE_TK = 1024
DENSE_UNROLL = 8
SUM_ROWS = 16


def _params(sem, vmem_limit=VMEM_LIMIT, flags=None):
    return pltpu.CompilerParams(dimension_semantics=sem, vmem_limit_bytes=vmem_limit, flags=flags)


def _rmsnorm_kernel(x_ref, g_ref, o_ref):
    x = x_ref[...]
    ms = jnp.mean(x * x, axis=-1, keepdims=True)
    o_ref[...] = (x * lax.rsqrt(ms + EPS) * g_ref[...]).astype(o_ref.dtype)


def _rmsnorm_bf16(x, g, tm=512):
    b, s, d = x.shape
    return pl.pallas_call(
        _rmsnorm_kernel,
        grid=(b, s // tm),
        in_specs=[pl.BlockSpec((None, tm, d), lambda bi, i: (bi, i, 0)),
                  pl.BlockSpec((1, d), lambda bi, i: (0, 0))],
        out_specs=pl.BlockSpec((None, tm, d), lambda bi, i: (bi, i, 0)),
        out_shape=jax.ShapeDtypeStruct((b, s, d), BF16),
        compiler_params=_params(("parallel", "parallel")),
        name="rmsnorm_bf16",
    )(x, g.reshape(1, d).astype(F32))


def _proj_kernel(*refs, kind, out_kind, nh, rc, tm, pad_tiles, src_tiles):
    h_ref, w_ref = refs[0], refs[1]
    o_ref = refs[-1]
    if kind in ("rope", "qknorm"):
        cos_ref, sin_ref = refs[2], refs[3]
    if kind == "qknorm":
        gain_ref = refs[4]
        lane = lax.broadcasted_iota(jnp.int32, (rc, HEAD_DIM), 1)
        lo = (lane % 64) < 32

    def epilogue(x, c, s):
        if kind == "rope":
            return x * c + pltpu.roll(x, 64, 1) * s
        if kind == "qknorm":
            ms = jnp.mean(x * x, axis=-1, keepdims=True)
            y = x * lax.rsqrt(ms + EPS) * gain_ref[...]
            yr = jnp.where(lo, pltpu.roll(y, 96, 1), pltpu.roll(y, 32, 1))
            return y * c + yr * s
        if kind == "sigmoid":
            return 1.0 / (1.0 + jnp.exp(-x))
        return x

    def compute():
        for r in range(tm // rc):
            r0, r1 = r * rc, (r + 1) * rc
            acc = jnp.dot(h_ref[r0:r1, :], w_ref[...], preferred_element_type=F32)
            c = s = None
            if kind in ("rope", "qknorm"):
                c, s = cos_ref[r0:r1, :], sin_ref[r0:r1, :]
            for j in range(nh):
                y = epilogue(acc[:, j * HEAD_DIM:(j + 1) * HEAD_DIM], c, s)
                if out_kind == "heads":
                    o_ref[j, r0:r1, :] = y.astype(o_ref.dtype)
                elif out_kind == "natural":
                    o_ref[r0:r1, j * HEAD_DIM:(j + 1) * HEAD_DIM] = y.astype(o_ref.dtype)
                else:
                    o_ref[j, r] = y.T.astype(o_ref.dtype)

    if pad_tiles == 0:
        compute()
    else:
        i = pl.program_id(1)
        is_pad = jnp.logical_or(i < pad_tiles, i >= pad_tiles + src_tiles)

        @pl.when(is_pad)
        def _():
            o_ref[...] = jnp.zeros(o_ref.shape, o_ref.dtype)

        @pl.when(jnp.logical_not(is_pad))
        def _():
            compute()


def _proj(h, w, col0, n, *, kind, out_kind, out_dtype=BF16, tables=(), gain=None, pad=0,
          tm=1024, tn=1024, rc=256, name):
    b, s, d = h.shape
    tn = min(tn, n)
    assert col0 % tn == 0 and n % tn == 0
    cb0 = col0 // tn
    nh = tn // HEAD_DIM
    src_tiles = s // tm
    pad_tiles = pad // tm
    rows = s + 2 * pad

    def src(i):
        if pad_tiles == 0:
            return i
        return jnp.clip(i - pad_tiles, 0, src_tiles - 1)

    in_specs = [pl.BlockSpec((None, tm, d), lambda bi, i, j: (bi, src(i), 0)),
                pl.BlockSpec((d, tn), lambda bi, i, j: (0, cb0 + j))]
    args = [h, w]
    for t in tables:
        in_specs.append(pl.BlockSpec((tm, HEAD_DIM), lambda bi, i, j: (src(i), 0)))
        args.append(t)
    if gain is not None:
        in_specs.append(pl.BlockSpec((1, HEAD_DIM), lambda bi, i, j: (0, 0)))
        args.append(gain)

    if out_kind == "heads":
        out_shape = (b, n // HEAD_DIM, rows, HEAD_DIM)
        out_spec = pl.BlockSpec((None, nh, tm, HEAD_DIM), lambda bi, i, j: (bi, j, i, 0))
    elif out_kind == "natural":
        out_shape = (b, rows, n)
        out_spec = pl.BlockSpec((None, tm, tn), lambda bi, i, j: (bi, i, j))
    else:
        out_shape = (b, n // HEAD_DIM, rows // rc, HEAD_DIM, rc)
        out_spec = pl.BlockSpec((None, nh, tm // rc, HEAD_DIM, rc),
                                lambda bi, i, j: (bi, j, i, 0, 0))

    kern = functools.partial(_proj_kernel, kind=kind, out_kind=out_kind, nh=nh, rc=rc, tm=tm,
                             pad_tiles=pad_tiles, src_tiles=src_tiles)
    return pl.pallas_call(
        kern,
        grid=(b, rows // tm, n // tn),
        in_specs=in_specs,
        out_specs=out_spec,
        out_shape=jax.ShapeDtypeStruct(out_shape, out_dtype),
        compiler_params=_params(("parallel", "parallel", "arbitrary")),
        name=name,
    )(*args)


def _band_bias(t):
    qi = np.arange(t)[:, None]
    kj = np.arange(t + 2 * BAND_RADIUS)[None, :]
    ok = np.abs(kj - BAND_RADIUS - qi) <= BAND_RADIUS
    return np.where(ok, 0.0, NEG_BIG).astype(np.float32)


def _band_kernel(q_ref, kp_ref, kc_ref, kn_ref, vp_ref, vc_ref, vn_ref, bias_ref,
                 o_ref, lse_ref, *scratch, d, t, cm, seq_m):
    c = pl.program_id(2)
    rad = BAND_RADIUS
    win = t + 2 * rad

    def rows(ref, r, n):
        if d == 1:
            return ref[...]
        return ref[pl.ds(r, n, stride=d), :].astype(BF16)

    biases = []
    for u in range(cm // t):
        kpos = c * cm + u * t - rad + lax.broadcasted_iota(jnp.int32, (1, win), 1)
        kvalid = jnp.where((kpos >= 0) & (kpos < seq_m), 0.0, NEG_BIG)
        biases.append(bias_ref[...] + kvalid)

    for r in range(d):
        q = rows(q_ref, r, cm)
        kw = jnp.concatenate([rows(kp_ref, r, rad), rows(kc_ref, r, cm), rows(kn_ref, r, rad)], 0)
        vw = jnp.concatenate([rows(vp_ref, r, rad), rows(vc_ref, r, cm), rows(vn_ref, r, rad)], 0)
        for u in range(cm // t):
            m0 = u * t
            s = lax.dot_general(q[m0:m0 + t], kw[m0:m0 + win], (((1,), (1,)), ((), ())),
                                preferred_element_type=F32) + biases[u]
            m = jnp.max(s, axis=-1, keepdims=True)
            p = jnp.exp2(s - m)
            l = jnp.sum(p, axis=-1, keepdims=True)
            o = jnp.dot(p.astype(BF16), vw[m0:m0 + win], preferred_element_type=F32) / l
            lse = jnp.broadcast_to(m + jnp.log2(l), (t, HEAD_DIM))
            if d == 1:
                o_ref[m0:m0 + t, :] = o.astype(o_ref.dtype)
                lse_ref[m0:m0 + t, :] = lse
            else:
                o_scr, lse_scr = scratch
                o_scr[pl.ds(r + d * m0, t, stride=d), :] = o
                lse_scr[pl.ds(r + d * m0, t, stride=d), :] = lse
    if d > 1:
        o_ref[...] = scratch[0][...].astype(o_ref.dtype)
        lse_ref[...] = scratch[1][...]


def _banded_group(q, k, v, g, head0):
    b, _, s, _ = q.shape
    d = DILATIONS[g]
    hpg = HEADS_PER_GROUP_A
    cm = Q_CHUNK // d
    t = min(BAND_TILE, cm)
    halo = BAND_RADIUS * d
    per = Q_CHUNK // halo
    last = s // halo - 1
    blk = pl.BlockSpec((None, None, Q_CHUNK, HEAD_DIM), lambda bi, hg, c: (bi, head0 + hg, c, 0))
    prev = pl.BlockSpec((None, None, halo, HEAD_DIM),
                        lambda bi, hg, c: (bi, head0 + hg, jnp.maximum(c * per - 1, 0), 0))
    nxt = pl.BlockSpec((None, None, halo, HEAD_DIM),
                       lambda bi, hg, c: (bi, head0 + hg, jnp.minimum((c + 1) * per, last), 0))
    out = pl.BlockSpec((None, None, Q_CHUNK, HEAD_DIM), lambda bi, hg, c: (bi, hg, c, 0))
    scratch = [] if d == 1 else [pltpu.VMEM((Q_CHUNK, HEAD_DIM), F32)] * 2
    return pl.pallas_call(
        functools.partial(_band_kernel, d=d, t=t, cm=cm, seq_m=s // d),
        grid=(b, hpg, s // Q_CHUNK),
        in_specs=[blk, prev, blk, nxt, prev, blk, nxt,
                  pl.BlockSpec((t, t + 2 * BAND_RADIUS), lambda bi, hg, c: (0, 0))],
        out_specs=[out, out],
        out_shape=[jax.ShapeDtypeStruct((b, hpg, s, HEAD_DIM), BF16),
                   jax.ShapeDtypeStruct((b, hpg, s, HEAD_DIM), F32)],
        scratch_shapes=scratch,
        compiler_params=_params(("parallel", "parallel", "arbitrary")),
        name=f"banded_attention_d{d}",
    )(q, k, k, k, v, v, v, jnp.asarray(_band_bias(t)))


def _dense_kernel(q_ref, qn_ref, k_ref, vt_ref, o_ref, acc_ref, s_ref, mx_ref, *, tq, tk):
    n_chunks = k_ref.shape[0] // tk
    heads = range(GQA_GROUP)

    def strip(g):
        return slice(g * tq, (g + 1) * tq)

    acc_ref[...] = jnp.zeros(acc_ref.shape, F32)

    def scores(c, slot, g, ref=q_ref):
        k0 = pl.multiple_of(c * tk, tk)
        q = ref[:, g * HEAD_DIM:(g + 1) * HEAD_DIM]
        st = lax.dot_general(k_ref[pl.ds(k0, tk), :], q, (((1,), (1,)), ((), ())),
                             preferred_element_type=F32)
        s_ref[slot, :, strip(g)] = st
        mx_ref[slot, :, strip(g)] = jnp.max(st, axis=0, keepdims=True)

    ones_rows = jnp.ones((SUM_ROWS, tk), BF16)

    def update(c, slot, g, m):
        m_new = jnp.maximum(m, mx_ref[slot, :, strip(g)])
        alpha = jnp.exp2(m - m_new)
        p = jnp.exp2(s_ref[slot, :, strip(g)] - m_new).astype(BF16)
        vt1 = jnp.concatenate([vt_ref[c], ones_rows], axis=0)
        pv = jnp.dot(vt1, p, preferred_element_type=F32)
        acc_ref[:, strip(g)] = acc_ref[:, strip(g)] * alpha + pv
        return m_new

    @pl.when(pl.program_id(2) == 0)
    def _():
        for g in heads:
            scores(0, 0, g)

    def half(c_next, next_ref, c, slot, ms):
        out = []
        for g in heads:
            scores(c_next, 1 - slot, g, next_ref)
            out.append(update(c, slot, g, ms[g]))
        return tuple(out)

    unroll = math.gcd(DENSE_UNROLL, n_chunks)

    def trip(j, ms, last):
        c0 = unroll * j
        for u in range(unroll):
            if last and u == unroll - 1:
                ms = half(0, qn_ref, c0 + u, u % 2, ms)
            else:
                ms = half(c0 + u + 1, q_ref, c0 + u, u % 2, ms)
        return ms

    m0 = tuple(jnp.full((1, tq), NEG_BIG, F32) for _ in heads)
    n_trips = n_chunks // unroll
    ms = lax.fori_loop(0, n_trips - 1, functools.partial(trip, last=False), m0)
    trip(n_trips - 1, ms, last=True)
    for g in heads:
        o = acc_ref[0:HEAD_DIM, strip(g)] / acc_ref[HEAD_DIM:HEAD_DIM + 1, strip(g)]
        o_ref[:, g * HEAD_DIM:(g + 1) * HEAD_DIM] = o.T.astype(o_ref.dtype)


def _dense_attention(qb, kb, vbt, tq=DENSE_TQ):
    b, s, dq = qb.shape
    tk = vbt.shape[-1]
    assert math.gcd(DENSE_UNROLL, s // tk) % 2 == 0, "the two score slots alternate per chunk"
    gw = GQA_GROUP * HEAD_DIM
    n_tiles = s // tq
    return pl.pallas_call(
        functools.partial(_dense_kernel, tq=tq, tk=tk),
        grid=(b, N_KV_B, n_tiles),
        in_specs=[pl.BlockSpec((None, tq, gw), lambda bi, h, i: (bi, i, h)),
                  pl.BlockSpec((None, tq, gw),
                               lambda bi, h, i: (bi, jnp.minimum(i + 1, n_tiles - 1), h)),
                  pl.BlockSpec((None, None, s, HEAD_DIM), lambda bi, h, i: (bi, h, 0, 0)),
                  pl.BlockSpec((None, None, s // tk, HEAD_DIM, tk),
                               lambda bi, h, i: (bi, h, 0, 0, 0))],
        out_specs=pl.BlockSpec((None, tq, gw), lambda bi, h, i: (bi, i, h)),
        out_shape=jax.ShapeDtypeStruct((b, s, dq), BF16),
        scratch_shapes=[pltpu.VMEM((HEAD_DIM + SUM_ROWS, GQA_GROUP * tq), F32),
                        pltpu.VMEM((2, tk, GQA_GROUP * tq), F32),
                        pltpu.VMEM((2, 1, GQA_GROUP * tq), F32)],
        compiler_params=_params(("arbitrary", "arbitrary", "arbitrary")),
        name="dense_gqa_attention",
    )(qb, qb, kb, vbt)


def _merge_kernel(*refs, tm, rc, ng):
    o_refs, lse_refs = refs[0:ng], refs[ng:2 * ng]
    yb_ref, wa_ref, wb_ref, g_ref, out_ref, ya_ref = refs[2 * ng:]
    d = out_ref.shape[-1]

    for h in range(HEADS_PER_GROUP_A):
        lses = [lse_refs[g][h] for g in range(ng)]
        top = functools.reduce(jnp.maximum, lses)
        es = [jnp.exp2(x - top) for x in lses]
        num = sum(e * o_refs[g][h].astype(F32) for g, e in enumerate(es))
        ya_ref[:, h * HEAD_DIM:(h + 1) * HEAD_DIM] = (num / sum(es)).astype(ya_ref.dtype)

    for r in range(tm // rc):
        r0, r1 = r * rc, (r + 1) * rc
        pa = jnp.dot(ya_ref[r0:r1, :], wa_ref[...], preferred_element_type=F32)
        pb = jnp.dot(yb_ref[r0:r1, :], wb_ref[...], preferred_element_type=F32)
        sa = g_ref[r0:r1, 0:d].astype(F32)
        sb = g_ref[r0:r1, d:2 * d].astype(F32)
        out_ref[r0:r1, :] = (sa * pa + sb * pb).astype(out_ref.dtype)


def _gated_merge(group_outs, group_lses, yb, wa, wb, gates, tm=512, rc=256):
    ng = len(group_outs)
    b, hpg, s, _ = group_outs[0].shape
    da = hpg * HEAD_DIM
    db = yb.shape[-1]
    d = wa.shape[1]
    hblk = pl.BlockSpec((None, hpg, tm, HEAD_DIM), lambda bi, i: (bi, 0, i, 0))
    once = pl.Buffered(1)
    return pl.pallas_call(
        functools.partial(_merge_kernel, tm=tm, rc=rc, ng=ng),
        grid=(b, s // tm),
        in_specs=[hblk] * (2 * ng) + [
            pl.BlockSpec((None, tm, db), lambda bi, i: (bi, i, 0)),
            pl.BlockSpec((da, d), lambda bi, i: (0, 0), pipeline_mode=once),
            pl.BlockSpec((db, d), lambda bi, i: (0, 0), pipeline_mode=once),
            pl.BlockSpec((None, tm, 2 * d), lambda bi, i: (bi, i, 0))],
        out_specs=pl.BlockSpec((None, tm, d), lambda bi, i: (bi, i, 0)),
        out_shape=jax.ShapeDtypeStruct((b, s, d), BF16),
        scratch_shapes=[pltpu.VMEM((tm, da), BF16)],
        compiler_params=_params(("parallel", "parallel")),
        name="gated_merge",
    )(*group_outs, *group_lses, yb, wa, wb, gates)


def _oproj_kernel(m_ref, w_ref, x_ref, o_ref, *, tm, rc):
    for r in range(tm // rc):
        r0, r1 = r * rc, (r + 1) * rc
        o_ref[r0:r1, :] = x_ref[r0:r1, :] + jnp.dot(m_ref[r0:r1, :], w_ref[...],
                                                    preferred_element_type=F32)


def _oproj_residual(merged, w_o, x, tm=1024, tn=1024, rc=256):
    b, s, d = merged.shape
    return pl.pallas_call(
        functools.partial(_oproj_kernel, tm=tm, rc=rc),
        grid=(b, s // tm, d // tn),
        in_specs=[pl.BlockSpec((None, tm, d), lambda bi, i, j: (bi, i, 0)),
                  pl.BlockSpec((d, tn), lambda bi, i, j: (0, j)),
                  pl.BlockSpec((None, tm, tn), lambda bi, i, j: (bi, i, j))],
        out_specs=pl.BlockSpec((None, tm, tn), lambda bi, i, j: (bi, i, j)),
        out_shape=jax.ShapeDtypeStruct((b, s, d), F32),
        compiler_params=_params(("parallel", "parallel", "arbitrary")),
        name="oproj_residual",
    )(merged, w_o, x)


def _ffn_kernel(x_ref, gffn_ref, gfin_ref, wg_ref, wu_ref, wd_ref, o_ref, h_ref, *, tm, rc):
    f = pl.program_id(2)

    @pl.when(f == 0)
    def _():
        x = x_ref[...]
        ms = jnp.mean(x * x, axis=-1, keepdims=True)
        h_ref[...] = (x * lax.rsqrt(ms + EPS) * gffn_ref[...]).astype(h_ref.dtype)
        o_ref[...] = jnp.zeros(o_ref.shape, F32)

    for r in range(tm // rc):
        r0, r1 = r * rc, (r + 1) * rc
        h = h_ref[r0:r1, :]
        gate = jnp.dot(h, wg_ref[...], preferred_element_type=F32)
        up = jnp.dot(h, wu_ref[...], preferred_element_type=F32)
        act = (gate / (1.0 + jnp.exp(-gate))) * up
        o_ref[r0:r1, :] += jnp.dot(act.astype(BF16), wd_ref[...], preferred_element_type=F32)

    @pl.when(f == pl.num_programs(2) - 1)
    def _():
        y = x_ref[...] + o_ref[...]
        ms = jnp.mean(y * y, axis=-1, keepdims=True)
        o_ref[...] = y * lax.rsqrt(ms + EPS) * gfin_ref[...]


def _ffn_final(x, g_ffn, g_final, w_gu, w_down, tm=1024, tf=512, rc=512):
    b, s, d = x.shape
    dff = w_down.shape[0]
    nf = dff // tf
    return pl.pallas_call(
        functools.partial(_ffn_kernel, tm=tm, rc=rc),
        grid=(b, s // tm, nf),
        in_specs=[pl.BlockSpec((None, tm, d), lambda bi, i, f: (bi, i, 0)),
                  pl.BlockSpec((1, d), lambda bi, i, f: (0, 0)),
                  pl.BlockSpec((1, d), lambda bi, i, f: (0, 0)),
                  pl.BlockSpec((d, tf), lambda bi, i, f: (0, f)),
                  pl.BlockSpec((d, tf), lambda bi, i, f: (0, nf + f)),
                  pl.BlockSpec((tf, d), lambda bi, i, f: (f, 0))],
        out_specs=pl.BlockSpec((None, tm, d), lambda bi, i, f: (bi, i, 0)),
        out_shape=jax.ShapeDtypeStruct((b, s, d), F32),
        scratch_shapes=[pltpu.VMEM((tm, d), BF16)],
        compiler_params=_params(("parallel", "parallel", "arbitrary"), vmem_limit=FFN_VMEM_LIMIT),
        name="ffn_final",
    )(x, g_ffn.reshape(1, d).astype(F32), g_final.reshape(1, d).astype(F32), w_gu, w_gu, w_down)


def _rope_tables(s):
    half = HEAD_DIM // 2
    inv = ROPE_THETA ** (-(jnp.arange(half, dtype=F32) * 2.0 / HEAD_DIM))
    ang = jnp.arange(s).astype(F32)[:, None] * inv[None, :]
    cos, sin = jnp.cos(ang), jnp.sin(ang)
    return jnp.concatenate([cos, cos], axis=-1), jnp.concatenate([-sin, sin], axis=-1)


def _axial_tables(s):
    dim = HEAD_DIM // 2
    half = dim // 2
    inv = ROPE_THETA ** (-(jnp.arange(half, dtype=F32) * 2.0 / dim))
    n_rows = s // GRID_W
    row = jnp.repeat(jnp.arange(n_rows), GRID_W).astype(F32)
    col = jnp.tile(jnp.arange(GRID_W), n_rows).astype(F32)
    cs, sn = [], []
    for pos in (row, col):
        ang = pos[:, None] * inv[None, :]
        c, sgn = jnp.cos(ang), jnp.sin(ang)
        cs += [c, c]
        sn += [-sgn, sgn]
    return jnp.concatenate(cs, axis=-1), jnp.concatenate(sn, axis=-1)


def _trunk(x, p):
    b, s, d = x.shape
    a_w = N_HEADS_A * HEAD_DIM
    b_q = d
    b_kv = N_KV_B * HEAD_DIM
    cuts = np.cumsum([0, a_w, a_w, a_w, b_q, b_kv, b_kv, 2 * d]).tolist()
    w_in = p["w_in"]

    def wcols(k):
        return w_in, cuts[k], cuts[k + 1] - cuts[k]

    h = _rmsnorm_bf16(x, p["g_attn"])
    cos1, sin1 = p["rope"]
    cos2, sin2 = p["axial"]

    gw = HEADS_PER_GROUP_A * HEAD_DIM
    qkv_a = []
    for col0, n, dt, tag in ((0, gw, BF16, "g0"), (gw, a_w - gw, F32, "g12")):
        qkv_a.append((
            _proj(h, w_in, cuts[0] + col0, n, kind="rope", out_kind="heads", out_dtype=dt,
                  tables=(cos1 * QSCALE, sin1 * QSCALE), name="proj_qa_" + tag),
            _proj(h, w_in, cuts[1] + col0, n, kind="rope", out_kind="heads", out_dtype=dt,
                  tables=(cos1, sin1), name="proj_ka_" + tag),
            _proj(h, w_in, cuts[2] + col0, n, kind="plain", out_kind="heads", out_dtype=dt,
                  name="proj_va_" + tag)))
    qb = _proj(h, *wcols(3), kind="qknorm", out_kind="natural", tables=(cos2, sin2),
               gain=p["q_gain"] * QSCALE, name="proj_qb")
    kb = _proj(h, *wcols(4), kind="qknorm", out_kind="heads", tables=(cos2, sin2),
               gain=p["k_gain"], name="proj_kb")
    vbt = _proj(h, *wcols(5), kind="plain", out_kind="vT", rc=DENSE_TK, name="proj_vb")
    gates = _proj(h, *wcols(6), kind="sigmoid", out_kind="natural", name="proj_gates")

    groups = [_banded_group(*qkv_a[0], 0, 0)]
    groups += [_banded_group(*qkv_a[1], g, (g - 1) * HEADS_PER_GROUP_A)
               for g in range(1, len(DILATIONS))]
    yb = _dense_attention(qb, kb, vbt)
    merged = _gated_merge([o for o, _ in groups], [l for _, l in groups], yb,
                          p["w_a_br"], p["w_b_br"], gates)
    x1 = _oproj_residual(merged, p["w_o"], x)
    return _ffn_final(x1, p["g_ffn"], p["g_final"], p["w_gate_up"], p["w_down"])


def kernel(x_prompt, x_sample, g_attn, w_in, q_gain_b, k_gain_b, w_a_br, w_b_br, w_o, g_ffn,
           w_gate_up, w_down, g_final):
    assert g_attn.shape[0] == 1, "single-layer trunk"
    s = x_prompt.shape[1]
    assert x_sample.shape[1] == s and s % Q_CHUNK == 0
    p = {
        "g_attn": g_attn[0], "g_ffn": g_ffn[0], "g_final": g_final,
        "w_in": w_in[0].astype(BF16),
        "q_gain": q_gain_b[0].reshape(1, HEAD_DIM).astype(F32),
        "k_gain": k_gain_b[0].reshape(1, HEAD_DIM).astype(F32),
        "w_a_br": w_a_br[0].astype(BF16), "w_b_br": w_b_br[0].astype(BF16),
        "w_o": w_o[0].astype(BF16),
        "w_gate_up": w_gate_up[0].astype(BF16), "w_down": w_down[0].astype(BF16),
        "rope": _rope_tables(s), "axial": _axial_tables(s),
    }
    return (_trunk(x_prompt, p), _trunk(x_sample, p))
```

```python
import functools
import math

import numpy as np
import jax
import jax.numpy as jnp
from jax import lax
from jax.experimental import pallas as pl
from jax.experimental.pallas import tpu as pltpu

HEAD_DIM = 128
DILATIONS = (1, 4, 16)
BAND_RADIUS = 64
HEADS_PER_GROUP_A = 8
N_HEADS_A = HEADS_PER_GROUP_A * len(DILATIONS)
N_KV_B = 4
GQA_GROUP = 4
GRID_W = 64
ROPE_THETA = 10000.0
EPS = 1e-6
NEG_BIG = -1e30

LANES = 128
VMEM_BYTES_V7X = 64 * 1024 * 1024
VMEM_LIMIT = VMEM_BYTES_V7X - 8 * 1024 * 1024
FFN_VMEM_LIMIT = VMEM_BYTES_V7X - 3 * 1024 * 1024

F32 = jnp.float32
BF16 = jnp.bfloat16

QSCALE = (HEAD_DIM ** -0.5) * math.log2(math.e)

KV_PAD = BAND_RADIUS * max(DILATIONS)
Q_CHUNK = 2 * KV_PAD
BAND_TILE = 128

DENSE_TQ = 256
DENSE_TK = 1024
DENSE_UNROLL = 8
SUM_ROWS = 16


def _params(sem, vmem_limit=VMEM_LIMIT, flags=None):
    return pltpu.CompilerParams(dimension_semantics=sem, vmem_limit_bytes=vmem_limit, flags=flags)


def _rmsnorm_kernel(x_ref, g_ref, o_ref):
    x = x_ref[...]
    ms = jnp.mean(x * x, axis=-1, keepdims=True)
    o_ref[...] = (x * lax.rsqrt(ms + EPS) * g_ref[...]).astype(o_ref.dtype)


def _rmsnorm_bf16(x, g, tm=512):
    b, s, d = x.shape
    return pl.pallas_call(
        _rmsnorm_kernel,
        grid=(b, s // tm),
        in_specs=[pl.BlockSpec((None, tm, d), lambda bi, i: (bi, i, 0)),
                  pl.BlockSpec((1, d), lambda bi, i: (0, 0))],
        out_specs=pl.BlockSpec((None, tm, d), lambda bi, i: (bi, i, 0)),
        out_shape=jax.ShapeDtypeStruct((b, s, d), BF16),
        compiler_params=_params(("parallel", "parallel")),
        name="rmsnorm_bf16",
    )(x, g.reshape(1, d).astype(F32))


def _proj_kernel(*refs, kind, out_kind, nh, rc, tm, pad_tiles, src_tiles):
    h_ref, w_ref = refs[0], refs[1]
    o_ref = refs[-1]
    if kind in ("rope", "qknorm"):
        cos_ref, sin_ref = refs[2], refs[3]
    if kind == "qknorm":
        gain_ref = refs[4]
        lane = lax.broadcasted_iota(jnp.int32, (rc, HEAD_DIM), 1)
        lo = (lane % 64) < 32

    def epilogue(x, c, s):
        if kind == "rope":
            return x * c + pltpu.roll(x, 64, 1) * s
        if kind == "qknorm":
            ms = jnp.mean(x * x, axis=-1, keepdims=True)
            y = x * lax.rsqrt(ms + EPS) * gain_ref[...]
            yr = jnp.where(lo, pltpu.roll(y, 96, 1), pltpu.roll(y, 32, 1))
            return y * c + yr * s
        if kind == "sigmoid":
            return 1.0 / (1.0 + jnp.exp(-x))
        return x

    def compute():
        for r in range(tm // rc):
            r0, r1 = r * rc, (r + 1) * rc
            acc = jnp.dot(h_ref[r0:r1, :], w_ref[...], preferred_element_type=F32)
            c = s = None
            if kind in ("rope", "qknorm"):
                c, s = cos_ref[r0:r1, :], sin_ref[r0:r1, :]
            for j in range(nh):
                y = epilogue(acc[:, j * HEAD_DIM:(j + 1) * HEAD_DIM], c, s)
                if out_kind == "heads":
                    o_ref[j, r0:r1, :] = y.astype(o_ref.dtype)
                elif out_kind == "natural":
                    o_ref[r0:r1, j * HEAD_DIM:(j + 1) * HEAD_DIM] = y.astype(o_ref.dtype)
                else:
                    o_ref[j, r] = y.T.astype(o_ref.dtype)

    if pad_tiles == 0:
        compute()
    else:
        i = pl.program_id(1)
        is_pad = jnp.logical_or(i < pad_tiles, i >= pad_tiles + src_tiles)

        @pl.when(is_pad)
        def _():
            o_ref[...] = jnp.zeros(o_ref.shape, o_ref.dtype)

        @pl.when(jnp.logical_not(is_pad))
        def _():
            compute()


def _proj(h, w, col0, n, *, kind, out_kind, out_dtype=BF16, tables=(), gain=None, pad=0,
          tm=2048, tn=1024, rc=256, name):
    b, s, d = h.shape
    tn = min(tn, n)
    assert col0 % tn == 0 and n % tn == 0
    cb0 = col0 // tn
    nh = tn // HEAD_DIM
    src_tiles = s // tm
    pad_tiles = pad // tm
    rows = s + 2 * pad

    def src(i):
        if pad_tiles == 0:
            return i
        return jnp.clip(i - pad_tiles, 0, src_tiles - 1)

    in_specs = [pl.BlockSpec((None, tm, d), lambda bi, i, j: (bi, src(i), 0)),
                pl.BlockSpec((d, tn), lambda bi, i, j: (0, cb0 + j))]
    args = [h, w]
    for t in tables:
        in_specs.append(pl.BlockSpec((tm, HEAD_DIM), lambda bi, i, j: (src(i), 0)))
        args.append(t)
    if gain is not None:
        in_specs.append(pl.BlockSpec((1, HEAD_DIM), lambda bi, i, j: (0, 0)))
        args.append(gain)

    if out_kind == "heads":
        out_shape = (b, n // HEAD_DIM, rows, HEAD_DIM)
        out_spec = pl.BlockSpec((None, nh, tm, HEAD_DIM), lambda bi, i, j: (bi, j, i, 0))
    elif out_kind == "natural":
        out_shape = (b, rows, n)
        out_spec = pl.BlockSpec((None, tm, tn), lambda bi, i, j: (bi, i, j))
    else:
        out_shape = (b, n // HEAD_DIM, rows // rc, HEAD_DIM, rc)
        out_spec = pl.BlockSpec((None, nh, tm // rc, HEAD_DIM, rc),
                                lambda bi, i, j: (bi, j, i, 0, 0))

    kern = functools.partial(_proj_kernel, kind=kind, out_kind=out_kind, nh=nh, rc=rc, tm=tm,
                             pad_tiles=pad_tiles, src_tiles=src_tiles)
    return pl.pallas_call(
        kern,
        grid=(b, rows // tm, n // tn),
        in_specs=in_specs,
        out_specs=out_spec,
        out_shape=jax.ShapeDtypeStruct(out_shape, out_dtype),
        compiler_params=_params(("parallel", "parallel", "arbitrary")),
        name=name,
    )(*args)


def _band_bias(t):
    qi = np.arange(t)[:, None]
    kj = np.arange(t + 2 * BAND_RADIUS)[None, :]
    ok = np.abs(kj - BAND_RADIUS - qi) <= BAND_RADIUS
    return np.where(ok, 0.0, NEG_BIG).astype(np.float32)


def _band_kernel(q_ref, kp_ref, kc_ref, kn_ref, vp_ref, vc_ref, vn_ref, bias_ref,
                 o_ref, lse_ref, *scratch, d, t, cm, seq_m):
    c = pl.program_id(2)
    rad = BAND_RADIUS
    win = t + 2 * rad

    def rows(ref, r, n):
        if d == 1:
            return ref[...]
        return ref[pl.ds(r, n, stride=d), :].astype(BF16)

    biases = []
    for u in range(cm // t):
        kpos = c * cm + u * t - rad + lax.broadcasted_iota(jnp.int32, (1, win), 1)
        kvalid = jnp.where((kpos >= 0) & (kpos < seq_m), 0.0, NEG_BIG)
        biases.append(bias_ref[...] + kvalid)

    for r in range(d):
        q = rows(q_ref, r, cm)
        kw = jnp.concatenate([rows(kp_ref, r, rad), rows(kc_ref, r, cm), rows(kn_ref, r, rad)], 0)
        vw = jnp.concatenate([rows(vp_ref, r, rad), rows(vc_ref, r, cm), rows(vn_ref, r, rad)], 0)
        for u in range(cm // t):
            m0 = u * t
            s = lax.dot_general(q[m0:m0 + t], kw[m0:m0 + win], (((1,), (1,)), ((), ())),
                                preferred_element_type=F32) + biases[u]
            m = jnp.max(s, axis=-1, keepdims=True)
            p = jnp.exp2(s - m)
            l = jnp.sum(p, axis=-1, keepdims=True)
            o = jnp.dot(p.astype(BF16), vw[m0:m0 + win], preferred_element_type=F32) / l
            lse = jnp.broadcast_to(m + jnp.log2(l), (t, HEAD_DIM))
            if d == 1:
                o_ref[m0:m0 + t, :] = o.astype(o_ref.dtype)
                lse_ref[m0:m0 + t, :] = lse
            else:
                o_scr, lse_scr = scratch
                o_scr[pl.ds(r + d * m0, t, stride=d), :] = o
                lse_scr[pl.ds(r + d * m0, t, stride=d), :] = lse
    if d > 1:
        o_ref[...] = scratch[0][...].astype(o_ref.dtype)
        lse_ref[...] = scratch[1][...]


def _banded_group(q, k, v, g, head0):
    b, _, s, _ = q.shape
    d = DILATIONS[g]
    hpg = HEADS_PER_GROUP_A
    cm = Q_CHUNK // d
    t = min(BAND_TILE, cm)
    halo = BAND_RADIUS * d
    per = Q_CHUNK // halo
    last = s // halo - 1
    blk = pl.BlockSpec((None, None, Q_CHUNK, HEAD_DIM), lambda bi, hg, c: (bi, head0 + hg, c, 0))
    prev = pl.BlockSpec((None, None, halo, HEAD_DIM),
                        lambda bi, hg, c: (bi, head0 + hg, jnp.maximum(c * per - 1, 0), 0))
    nxt = pl.BlockSpec((None, None, halo, HEAD_DIM),
                       lambda bi, hg, c: (bi, head0 + hg, jnp.minimum((c + 1) * per, last), 0))
    out = pl.BlockSpec((None, None, Q_CHUNK, HEAD_DIM), lambda bi, hg, c: (bi, hg, c, 0))
    scratch = [] if d == 1 else [pltpu.VMEM((Q_CHUNK, HEAD_DIM), F32)] * 2
    return pl.pallas_call(
        functools.partial(_band_kernel, d=d, t=t, cm=cm, seq_m=s // d),
        grid=(b, hpg, s // Q_CHUNK),
        in_specs=[blk, prev, blk, nxt, prev, blk, nxt,
                  pl.BlockSpec((t, t + 2 * BAND_RADIUS), lambda bi, hg, c: (0, 0))],
        out_specs=[out, out],
        out_shape=[jax.ShapeDtypeStruct((b, hpg, s, HEAD_DIM), BF16),
                   jax.ShapeDtypeStruct((b, hpg, s, HEAD_DIM), F32)],
        scratch_shapes=scratch,
        compiler_params=_params(("parallel", "parallel", "arbitrary")),
        name=f"banded_attention_d{d}",
    )(q, k, k, k, v, v, v, jnp.asarray(_band_bias(t)))


def _dense_kernel(q_ref, qn_ref, k_ref, vt_ref, o_ref, acc_ref, s_ref, mx_ref, *, tq, tk):
    n_chunks = k_ref.shape[0] // tk
    heads = range(GQA_GROUP)

    def strip(g):
        return slice(g * tq, (g + 1) * tq)

    acc_ref[...] = jnp.zeros(acc_ref.shape, F32)

    def scores(c, slot, g, ref=q_ref):
        k0 = pl.multiple_of(c * tk, tk)
        q = ref[:, g * HEAD_DIM:(g + 1) * HEAD_DIM]
        st = lax.dot_general(k_ref[pl.ds(k0, tk), :], q, (((1,), (1,)), ((), ())),
                             preferred_element_type=F32)
        s_ref[slot, :, strip(g)] = st
        mx_ref[slot, :, strip(g)] = jnp.max(st, axis=0, keepdims=True)

    ones_rows = jnp.ones((SUM_ROWS, tk), BF16)

    def update(c, slot, g, m):
        m_new = jnp.maximum(m, mx_ref[slot, :, strip(g)])
        alpha = jnp.exp2(m - m_new)
        p = jnp.exp2(s_ref[slot, :, strip(g)] - m_new).astype(BF16)
        vt1 = jnp.concatenate([vt_ref[c], ones_rows], axis=0)
        pv = jnp.dot(vt1, p, preferred_element_type=F32)
        acc_ref[:, strip(g)] = acc_ref[:, strip(g)] * alpha + pv
        return m_new

    @pl.when(pl.program_id(2) == 0)
    def _():
        for g in heads:
            scores(0, 0, g)

    def half(c_next, next_ref, c, slot, ms):
        out = []
        for g in heads:
            scores(c_next, 1 - slot, g, next_ref)
            out.append(update(c, slot, g, ms[g]))
        return tuple(out)

    unroll = math.gcd(DENSE_UNROLL, n_chunks)

    def trip(j, ms, last):
        c0 = unroll * j
        for u in range(unroll):
            if last and u == unroll - 1:
                ms = half(0, qn_ref, c0 + u, u % 2, ms)
            else:
                ms = half(c0 + u + 1, q_ref, c0 + u, u % 2, ms)
        return ms

    m0 = tuple(jnp.full((1, tq), NEG_BIG, F32) for _ in heads)
    n_trips = n_chunks // unroll
    ms = lax.fori_loop(0, n_trips - 1, functools.partial(trip, last=False), m0)
    trip(n_trips - 1, ms, last=True)
    for g in heads:
        o = acc_ref[0:HEAD_DIM, strip(g)] / acc_ref[HEAD_DIM:HEAD_DIM + 1, strip(g)]
        o_ref[:, g * HEAD_DIM:(g + 1) * HEAD_DIM] = o.T.astype(o_ref.dtype)


def _dense_attention(qb, kb, vbt, tq=DENSE_TQ):
    b, s, dq = qb.shape
    tk = vbt.shape[-1]
    assert math.gcd(DENSE_UNROLL, s // tk) % 2 == 0, "the two score slots alternate per chunk"
    gw = GQA_GROUP * HEAD_DIM
    n_tiles = s // tq
    return pl.pallas_call(
        functools.partial(_dense_kernel, tq=tq, tk=tk),
        grid=(b, N_KV_B, n_tiles),
        in_specs=[pl.BlockSpec((None, tq, gw), lambda bi, h, i: (bi, i, h)),
                  pl.BlockSpec((None, tq, gw),
                               lambda bi, h, i: (bi, jnp.minimum(i + 1, n_tiles - 1), h)),
                  pl.BlockSpec((None, None, s, HEAD_DIM), lambda bi, h, i: (bi, h, 0, 0)),
                  pl.BlockSpec((None, None, s // tk, HEAD_DIM, tk),
                               lambda bi, h, i: (bi, h, 0, 0, 0))],
        out_specs=pl.BlockSpec((None, tq, gw), lambda bi, h, i: (bi, i, h)),
        out_shape=jax.ShapeDtypeStruct((b, s, dq), BF16),
        scratch_shapes=[pltpu.VMEM((HEAD_DIM + SUM_ROWS, GQA_GROUP * tq), F32),
                        pltpu.VMEM((2, tk, GQA_GROUP * tq), F32),
                        pltpu.VMEM((2, 1, GQA_GROUP * tq), F32)],
        compiler_params=_params(("arbitrary", "arbitrary", "arbitrary")),
        name="dense_gqa_attention",
    )(qb, qb, kb, vbt)


def _merge_kernel(*refs, tm, rc, ng):
    o_refs, lse_refs = refs[0:ng], refs[ng:2 * ng]
    yb_ref, wa_ref, wb_ref, g_ref, out_ref, ya_ref = refs[2 * ng:]
    d = out_ref.shape[-1]

    for h in range(HEADS_PER_GROUP_A):
        lses = [lse_refs[g][h] for g in range(ng)]
        top = functools.reduce(jnp.maximum, lses)
        es = [jnp.exp2(x - top) for x in lses]
        num = sum(e * o_refs[g][h].astype(F32) for g, e in enumerate(es))
        ya_ref[:, h * HEAD_DIM:(h + 1) * HEAD_DIM] = (num / sum(es)).astype(ya_ref.dtype)

    for r in range(tm // rc):
        r0, r1 = r * rc, (r + 1) * rc
        pa = jnp.dot(ya_ref[r0:r1, :], wa_ref[...], preferred_element_type=F32)
        pb = jnp.dot(yb_ref[r0:r1, :], wb_ref[...], preferred_element_type=F32)
        sa = g_ref[r0:r1, 0:d].astype(F32)
        sb = g_ref[r0:r1, d:2 * d].astype(F32)
        out_ref[r0:r1, :] = (sa * pa + sb * pb).astype(out_ref.dtype)


def _gated_merge(group_outs, group_lses, yb, wa, wb, gates, tm=512, rc=256):
    ng = len(group_outs)
    b, hpg, s, _ = group_outs[0].shape
    da = hpg * HEAD_DIM
    db = yb.shape[-1]
    d = wa.shape[1]
    hblk = pl.BlockSpec((None, hpg, tm, HEAD_DIM), lambda bi, i: (bi, 0, i, 0))
    once = pl.Buffered(1)
    return pl.pallas_call(
        functools.partial(_merge_kernel, tm=tm, rc=rc, ng=ng),
        grid=(b, s // tm),
        in_specs=[hblk] * (2 * ng) + [
            pl.BlockSpec((None, tm, db), lambda bi, i: (bi, i, 0)),
            pl.BlockSpec((da, d), lambda bi, i: (0, 0), pipeline_mode=once),
            pl.BlockSpec((db, d), lambda bi, i: (0, 0), pipeline_mode=once),
            pl.BlockSpec((None, tm, 2 * d), lambda bi, i: (bi, i, 0))],
        out_specs=pl.BlockSpec((None, tm, d), lambda bi, i: (bi, i, 0)),
        out_shape=jax.ShapeDtypeStruct((b, s, d), BF16),
        scratch_shapes=[pltpu.VMEM((tm, da), BF16)],
        compiler_params=_params(("parallel", "parallel")),
        name="gated_merge",
    )(*group_outs, *group_lses, yb, wa, wb, gates)


def _oproj_kernel(m_ref, w_ref, x_ref, o_ref, *, tm, rc):
    for r in range(tm // rc):
        r0, r1 = r * rc, (r + 1) * rc
        o_ref[r0:r1, :] = x_ref[r0:r1, :] + jnp.dot(m_ref[r0:r1, :], w_ref[...],
                                                    preferred_element_type=F32)


def _oproj_residual(merged, w_o, x, tm=1024, tn=1024, rc=256):
    b, s, d = merged.shape
    return pl.pallas_call(
        functools.partial(_oproj_kernel, tm=tm, rc=rc),
        grid=(b, s // tm, d // tn),
        in_specs=[pl.BlockSpec((None, tm, d), lambda bi, i, j: (bi, i, 0)),
                  pl.BlockSpec((d, tn), lambda bi, i, j: (0, j)),
                  pl.BlockSpec((None, tm, tn), lambda bi, i, j: (bi, i, j))],
        out_specs=pl.BlockSpec((None, tm, tn), lambda bi, i, j: (bi, i, j)),
        out_shape=jax.ShapeDtypeStruct((b, s, d), F32),
        compiler_params=_params(("parallel", "parallel", "arbitrary")),
        name="oproj_residual",
    )(merged, w_o, x)


def _ffn_kernel(x_ref, gffn_ref, gfin_ref, wg_ref, wu_ref, wd_ref, o_ref, h_ref, *, tm, rc):
    f = pl.program_id(2)

    @pl.when(f == 0)
    def _():
        x = x_ref[...]
        ms = jnp.mean(x * x, axis=-1, keepdims=True)
        h_ref[...] = (x * lax.rsqrt(ms + EPS) * gffn_ref[...]).astype(h_ref.dtype)
        o_ref[...] = jnp.zeros(o_ref.shape, F32)

    for r in range(tm // rc):
        r0, r1 = r * rc, (r + 1) * rc
        h = h_ref[r0:r1, :]
        gate = jnp.dot(h, wg_ref[...], preferred_element_type=F32)
        up = jnp.dot(h, wu_ref[...], preferred_element_type=F32)
        act = (gate / (1.0 + jnp.exp(-gate))) * up
        o_ref[r0:r1, :] += jnp.dot(act.astype(BF16), wd_ref[...], preferred_element_type=F32)

    @pl.when(f == pl.num_programs(2) - 1)
    def _():
        y = x_ref[...] + o_ref[...]
        ms = jnp.mean(y * y, axis=-1, keepdims=True)
        o_ref[...] = y * lax.rsqrt(ms + EPS) * gfin_ref[...]


def _ffn_final(x, g_ffn, g_final, w_gu, w_down, tm=1024, tf=512, rc=512):
    b, s, d = x.shape
    dff = w_down.shape[0]
    nf = dff // tf
    return pl.pallas_call(
        functools.partial(_ffn_kernel, tm=tm, rc=rc),
        grid=(b, s // tm, nf),
        in_specs=[pl.BlockSpec((None, tm, d), lambda bi, i, f: (bi, i, 0)),
                  pl.BlockSpec((1, d), lambda bi, i, f: (0, 0)),
                  pl.BlockSpec((1, d), lambda bi, i, f: (0, 0)),
                  pl.BlockSpec((d, tf), lambda bi, i, f: (0, f)),
                  pl.BlockSpec((d, tf), lambda bi, i, f: (0, nf + f)),
                  pl.BlockSpec((tf, d), lambda bi, i, f: (f, 0))],
        out_specs=pl.BlockSpec((None, tm, d), lambda bi, i, f: (bi, i, 0)),
        out_shape=jax.ShapeDtypeStruct((b, s, d), F32),
        scratch_shapes=[pltpu.VMEM((tm, d), BF16)],
        compiler_params=_params(("parallel", "parallel", "arbitrary"), vmem_limit=FFN_VMEM_LIMIT),
        name="ffn_final",
    )(x, g_ffn.reshape(1, d).astype(F32), g_final.reshape(1, d).astype(F32), w_gu, w_gu, w_down)


def _rope_tables(s):
    half = HEAD_DIM // 2
    inv = ROPE_THETA ** (-(jnp.arange(half, dtype=F32) * 2.0 / HEAD_DIM))
    ang = jnp.arange(s).astype(F32)[:, None] * inv[None, :]
    cos, sin = jnp.cos(ang), jnp.sin(ang)
    return jnp.concatenate([cos, cos], axis=-1), jnp.concatenate([-sin, sin], axis=-1)


def _axial_tables(s):
    dim = HEAD_DIM // 2
    half = dim // 2
    inv = ROPE_THETA ** (-(jnp.arange(half, dtype=F32) * 2.0 / dim))
    n_rows = s // GRID_W
    row = jnp.repeat(jnp.arange(n_rows), GRID_W).astype(F32)
    col = jnp.tile(jnp.arange(GRID_W), n_rows).astype(F32)
    cs, sn = [], []
    for pos in (row, col):
        ang = pos[:, None] * inv[None, :]
        c, sgn = jnp.cos(ang), jnp.sin(ang)
        cs += [c, c]
        sn += [-sgn, sgn]
    return jnp.concatenate(cs, axis=-1), jnp.concatenate(sn, axis=-1)


def _trunk(x, p):
    b, s, d = x.shape
    a_w = N_HEADS_A * HEAD_DIM
    b_q = d
    b_kv = N_KV_B * HEAD_DIM
    cuts = np.cumsum([0, a_w, a_w, a_w, b_q, b_kv, b_kv, 2 * d]).tolist()
    w_in = p["w_in"]

    def wcols(k):
        return w_in, cuts[k], cuts[k + 1] - cuts[k]

    h = _rmsnorm_bf16(x, p["g_attn"])
    cos1, sin1 = p["rope"]
    cos2, sin2 = p["axial"]

    gw = HEADS_PER_GROUP_A * HEAD_DIM
    qkv_a = []
    for col0, n, dt, tag in ((0, gw, BF16, "g0"), (gw, a_w - gw, F32, "g12")):
        qkv_a.append((
            _proj(h, w_in, cuts[0] + col0, n, kind="rope", out_kind="heads", out_dtype=dt,
                  tables=(cos1 * QSCALE, sin1 * QSCALE), name="proj_qa_" + tag),
            _proj(h, w_in, cuts[1] + col0, n, kind="rope", out_kind="heads", out_dtype=dt,
                  tables=(cos1, sin1), name="proj_ka_" + tag),
            _proj(h, w_in, cuts[2] + col0, n, kind="plain", out_kind="heads", out_dtype=dt,
                  name="proj_va_" + tag)))
    qb = _proj(h, *wcols(3), kind="qknorm", out_kind="natural", tables=(cos2, sin2),
               gain=p["q_gain"] * QSCALE, name="proj_qb")
    kb = _proj(h, *wcols(4), kind="qknorm", out_kind="heads", tables=(cos2, sin2),
               gain=p["k_gain"], name="proj_kb")
    vbt = _proj(h, *wcols(5), kind="plain", out_kind="vT", rc=DENSE_TK, name="proj_vb")
    gates = _proj(h, *wcols(6), kind="sigmoid", out_kind="natural", name="proj_gates")

    groups = [_banded_group(*qkv_a[0], 0, 0)]
    groups += [_banded_group(*qkv_a[1], g, (g - 1) * HEADS_PER_GROUP_A)
               for g in range(1, len(DILATIONS))]
    yb = _dense_attention(qb, kb, vbt)
    merged = _gated_merge([o for o, _ in groups], [l for _, l in groups], yb,
                          p["w_a_br"], p["w_b_br"], gates)
    x1 = _oproj_residual(merged, p["w_o"], x)
    return _ffn_final(x1, p["g_ffn"], p["g_final"], p["w_gate_up"], p["w_down"])


def kernel(x_prompt, x_sample, g_attn, w_in, q_gain_b, k_gain_b, w_a_br, w_b_br, w_o, g_ffn,
           w_gate_up, w_down, g_final):
    assert g_attn.shape[0] == 1, "single-layer trunk"
    s = x_prompt.shape[1]
    assert x_sample.shape[1] == s and s % Q_CHUNK == 0
    p = {
        "g_attn": g_attn[0], "g_ffn": g_ffn[0], "g_final": g_final,
        "w_in": w_in[0].astype(BF16),
        "q_gain": q_gain_b[0].reshape(1, HEAD_DIM).astype(F32),
        "k_gain": k_gain_b[0].reshape(1, HEAD_DIM).astype(F32),
        "w_a_br": w_a_br[0].astype(BF16), "w_b_br": w_b_br[0].astype(BF16),
        "w_o": w_o[0].astype(BF16),
        "w_gate_up": w_gate_up[0].astype(BF16), "w_down": w_down[0].astype(BF16),
        "rope": _rope_tables(s), "axial": _axial_tables(s),
    }
    return (_trunk(x_prompt, p), _trunk(x_sample, p))
```

```python
import functools
import math

import numpy as np
import jax
import jax.numpy as jnp
from jax import lax
from jax.experimental import pallas as pl
from jax.experimental.pallas import tpu as pltpu

HEAD_DIM = 128
DILATIONS = (1, 4, 16)
BAND_RADIUS = 64
HEADS_PER_GROUP_A = 8
N_HEADS_A = HEADS_PER_GROUP_A * len(DILATIONS)
N_KV_B = 4
GQA_GROUP = 4
GRID_W = 64
ROPE_THETA = 10000.0
EPS = 1e-6
NEG_BIG = -1e30

VMEM_BYTES_V7X = 64 * 1024 * 1024
VMEM_LIMIT = VMEM_BYTES_V7X - 8 * 1024 * 1024
FFN_VMEM_LIMIT = VMEM_BYTES_V7X - 3 * 1024 * 1024

F32 = jnp.float32
BF16 = jnp.bfloat16

QSCALE = (HEAD_DIM ** -0.5) * math.log2(math.e)

Q_CHUNK = 4 * BAND_RADIUS * max(DILATIONS)
BAND_TILE = 128
SPLIT_STRIDE = 4

DENSE_TQ = 256
DENSE_TK = 1024
DENSE_UNROLL = 8
SUM_ROWS = 16


def _params(sem, vmem_limit=VMEM_LIMIT):
    return pltpu.CompilerParams(dimension_semantics=sem, vmem_limit_bytes=vmem_limit)


def _rmsnorm_kernel(x_ref, g_ref, o_ref):
    x = x_ref[...]
    ms = jnp.mean(x * x, axis=-1, keepdims=True)
    o_ref[...] = (x * lax.rsqrt(ms + EPS) * g_ref[...]).astype(o_ref.dtype)


def _rmsnorm_bf16(x, g, tm=512):
    b, s, d = x.shape
    return pl.pallas_call(
        _rmsnorm_kernel,
        grid=(b, s // tm),
        in_specs=[pl.BlockSpec((None, tm, d), lambda bi, i: (bi, i, 0)),
                  pl.BlockSpec((1, d), lambda bi, i: (0, 0))],
        out_specs=pl.BlockSpec((None, tm, d), lambda bi, i: (bi, i, 0)),
        out_shape=jax.ShapeDtypeStruct((b, s, d), BF16),
        compiler_params=_params(("parallel", "parallel")),
        name="rmsnorm_bf16",
    )(x, g.reshape(1, d).astype(F32))


def _proj_kernel(*refs, kind, out_kind, nh, rc, tm):
    h_ref, w_ref = refs[0], refs[1]
    o_ref = refs[-1]
    if kind in ("rope", "qknorm"):
        cos_ref, sin_ref = refs[2], refs[3]
    if kind == "qknorm":
        gain_ref = refs[4]
        lane = lax.broadcasted_iota(jnp.int32, (rc, HEAD_DIM), 1)
        lo = (lane % 64) < 32

    def epilogue(x, c, s):
        if kind == "rope":
            return x * c + pltpu.roll(x, 64, 1) * s
        if kind == "qknorm":
            ms = jnp.mean(x * x, axis=-1, keepdims=True)
            y = x * lax.rsqrt(ms + EPS) * gain_ref[...]
            yr = jnp.where(lo, pltpu.roll(y, 96, 1), pltpu.roll(y, 32, 1))
            return y * c + yr * s
        if kind == "sigmoid":
            return 1.0 / (1.0 + jnp.exp(-x))
        return x

    for r in range(tm // rc):
        r0, r1 = r * rc, (r + 1) * rc
        acc = jnp.dot(h_ref[r0:r1, :], w_ref[...], preferred_element_type=F32)
        c = s = None
        if kind in ("rope", "qknorm"):
            c, s = cos_ref[r0:r1, :], sin_ref[r0:r1, :]
        for j in range(nh):
            y = epilogue(acc[:, j * HEAD_DIM:(j + 1) * HEAD_DIM], c, s)
            if out_kind == "heads":
                o_ref[j, r0:r1, :] = y.astype(o_ref.dtype)
            elif out_kind == "natural":
                o_ref[r0:r1, j * HEAD_DIM:(j + 1) * HEAD_DIM] = y.astype(o_ref.dtype)
            else:
                o_ref[j, r] = y.T.astype(o_ref.dtype)


def _proj(h, w, col0, n, *, kind, out_kind, out_dtype=BF16, tables=(), gain=None,
          tm=2048, tn=1024, rc=256, name):
    b, s, d = h.shape
    tn = min(tn, n)
    assert col0 % tn == 0 and n % tn == 0 and s % tm == 0
    cb0 = col0 // tn
    nh = tn // HEAD_DIM

    in_specs = [pl.BlockSpec((None, tm, d), lambda bi, i, j: (bi, i, 0)),
                pl.BlockSpec((d, tn), lambda bi, i, j: (0, cb0 + j))]
    args = [h, w]
    for t in tables:
        in_specs.append(pl.BlockSpec((tm, HEAD_DIM), lambda bi, i, j: (i, 0)))
        args.append(t)
    if gain is not None:
        in_specs.append(pl.BlockSpec((1, HEAD_DIM), lambda bi, i, j: (0, 0)))
        args.append(gain)

    if out_kind == "heads":
        out_shape = (b, n // HEAD_DIM, s, HEAD_DIM)
        out_spec = pl.BlockSpec((None, nh, tm, HEAD_DIM), lambda bi, i, j: (bi, j, i, 0))
    elif out_kind == "natural":
        out_shape = (b, s, n)
        out_spec = pl.BlockSpec((None, tm, tn), lambda bi, i, j: (bi, i, j))
    else:
        out_shape = (b, n // HEAD_DIM, s // rc, HEAD_DIM, rc)
        out_spec = pl.BlockSpec((None, nh, tm // rc, HEAD_DIM, rc),
                                lambda bi, i, j: (bi, j, i, 0, 0))

    kern = functools.partial(_proj_kernel, kind=kind, out_kind=out_kind, nh=nh, rc=rc, tm=tm)
    return pl.pallas_call(
        kern,
        grid=(b, s // tm, n // tn),
        in_specs=in_specs,
        out_specs=out_spec,
        out_shape=jax.ShapeDtypeStruct(out_shape, out_dtype),
        compiler_params=_params(("parallel", "parallel", "arbitrary")),
        name=name,
    )(*args)


def _band_bias(t):
    qi = np.arange(t)[:, None]
    kj = np.arange(t + 2 * BAND_RADIUS)[None, :]
    ok = np.abs(kj - BAND_RADIUS - qi) <= BAND_RADIUS
    return np.where(ok, 0.0, NEG_BIG).astype(np.float32)


def _band_kernel(q_ref, kp_ref, kc_ref, kn_ref, vp_ref, vc_ref, vn_ref, bias_ref,
                 o_ref, lse_ref, *scratch, d, t, cm, seq_m):
    c = pl.program_id(2)
    rad = BAND_RADIUS
    win = t + 2 * rad
    operands = (q_ref, kp_ref, kc_ref, kn_ref, vp_ref, vc_ref, vn_ref)
    stages = scratch[2:]

    two_pass = d > SPLIT_STRIDE
    if two_pass:
        for ref, stage in zip(operands, stages):
            part = ref.shape[0] // SPLIT_STRIDE
            for r4 in range(SPLIT_STRIDE):
                stage[r4 * part:(r4 + 1) * part, :] = ref[pl.ds(r4, part, stride=SPLIT_STRIDE), :]

    def rows(i, r, n):
        ref = operands[i]
        if d == 1:
            return ref[...]
        if not two_pass:
            return ref[pl.ds(r, n, stride=d), :].astype(BF16)
        part = ref.shape[0] // SPLIT_STRIDE
        start = (r % SPLIT_STRIDE) * part + r // SPLIT_STRIDE
        return stages[i][pl.ds(start, n, stride=d // SPLIT_STRIDE), :].astype(BF16)

    biases = []
    for u in range(cm // t):
        kpos = c * cm + u * t - rad + lax.broadcasted_iota(jnp.int32, (1, win), 1)
        kvalid = jnp.where((kpos >= 0) & (kpos < seq_m), 0.0, NEG_BIG)
        biases.append(bias_ref[...] + kvalid)

    for r in range(d):
        q = rows(0, r, cm)
        kw = jnp.concatenate([rows(1, r, rad), rows(2, r, cm), rows(3, r, rad)], axis=0)
        vw = jnp.concatenate([rows(4, r, rad), rows(5, r, cm), rows(6, r, rad)], axis=0)
        for u in range(cm // t):
            m0 = u * t
            s = lax.dot_general(q[m0:m0 + t], kw[m0:m0 + win], (((1,), (1,)), ((), ())),
                                preferred_element_type=F32) + biases[u]
            m = jnp.max(s, axis=-1, keepdims=True)
            p = jnp.exp2(s - m)
            l = jnp.sum(p, axis=-1, keepdims=True)
            o = jnp.dot(p.astype(BF16), vw[m0:m0 + win], preferred_element_type=F32) / l
            lse = jnp.broadcast_to(m + jnp.log2(l), (t, HEAD_DIM))
            if d == 1:
                o_ref[m0:m0 + t, :] = o.astype(o_ref.dtype)
                lse_ref[m0:m0 + t, :] = lse
            else:
                o_scr, lse_scr = scratch[0], scratch[1]
                o_scr[pl.ds(r + d * m0, t, stride=d), :] = o
                lse_scr[pl.ds(r + d * m0, t, stride=d), :] = lse
    if d > 1:
        o_ref[...] = scratch[0][...].astype(o_ref.dtype)
        lse_ref[...] = scratch[1][...]


def _banded_group(q, k, v, g, head0):
    b, _, s, _ = q.shape
    d = DILATIONS[g]
    hpg = HEADS_PER_GROUP_A
    cm = Q_CHUNK // d
    t = min(BAND_TILE, cm)
    halo = BAND_RADIUS * d
    per = Q_CHUNK // halo
    last = s // halo - 1
    blk = pl.BlockSpec((None, None, Q_CHUNK, HEAD_DIM), lambda bi, hg, c: (bi, head0 + hg, c, 0))
    prev = pl.BlockSpec((None, None, halo, HEAD_DIM),
                        lambda bi, hg, c: (bi, head0 + hg, jnp.maximum(c * per - 1, 0), 0))
    nxt = pl.BlockSpec((None, None, halo, HEAD_DIM),
                       lambda bi, hg, c: (bi, head0 + hg, jnp.minimum((c + 1) * per, last), 0))
    out = pl.BlockSpec((None, None, Q_CHUNK, HEAD_DIM), lambda bi, hg, c: (bi, hg, c, 0))
    scratch = [] if d == 1 else [pltpu.VMEM((Q_CHUNK, HEAD_DIM), F32)] * 2
    if d > SPLIT_STRIDE:
        scratch += [pltpu.VMEM((n, HEAD_DIM), F32) for n in (Q_CHUNK,) + (halo, Q_CHUNK, halo) * 2]
    return pl.pallas_call(
        functools.partial(_band_kernel, d=d, t=t, cm=cm, seq_m=s // d),
        grid=(b, hpg, s // Q_CHUNK),
        in_specs=[blk, prev, blk, nxt, prev, blk, nxt,
                  pl.BlockSpec((t, t + 2 * BAND_RADIUS), lambda bi, hg, c: (0, 0))],
        out_specs=[out, out],
        out_shape=[jax.ShapeDtypeStruct((b, hpg, s, HEAD_DIM), BF16),
                   jax.ShapeDtypeStruct((b, hpg, s, HEAD_DIM), F32)],
        scratch_shapes=scratch,
        compiler_params=_params(("parallel", "parallel", "arbitrary")),
        name=f"banded_attention_d{d}",
    )(q, k, k, k, v, v, v, jnp.asarray(_band_bias(t)))


def _dense_kernel(q_ref, qn_ref, k_ref, vt_ref, o_ref, acc_ref, s_ref, mx_ref, *, tq, tk):
    n_chunks = k_ref.shape[0] // tk
    heads = range(GQA_GROUP)

    def strip(g):
        return slice(g * tq, (g + 1) * tq)

    acc_ref[...] = jnp.zeros(acc_ref.shape, F32)

    def scores(c, slot, g, ref=q_ref):
        k0 = pl.multiple_of(c * tk, tk)
        q = ref[:, g * HEAD_DIM:(g + 1) * HEAD_DIM]
        st = lax.dot_general(k_ref[pl.ds(k0, tk), :], q, (((1,), (1,)), ((), ())),
                             preferred_element_type=F32)
        s_ref[slot, :, strip(g)] = st
        mx_ref[slot, :, strip(g)] = jnp.max(st, axis=0, keepdims=True)

    ones_rows = jnp.ones((SUM_ROWS, tk), BF16)

    def update(c, slot, g, m):
        m_new = jnp.maximum(m, mx_ref[slot, :, strip(g)])
        alpha = jnp.exp2(m - m_new)
        p = jnp.exp2(s_ref[slot, :, strip(g)] - m_new).astype(BF16)
        vt1 = jnp.concatenate([vt_ref[c], ones_rows], axis=0)
        pv = jnp.dot(vt1, p, preferred_element_type=F32)
        acc_ref[:, strip(g)] = acc_ref[:, strip(g)] * alpha + pv
        return m_new

    @pl.when(pl.program_id(2) == 0)
    def _():
        for g in heads:
            scores(0, 0, g)

    def half(c_next, next_ref, c, slot, ms):
        out = []
        for g in heads:
            scores(c_next, 1 - slot, g, next_ref)
            out.append(update(c, slot, g, ms[g]))
        return tuple(out)

    unroll = math.gcd(DENSE_UNROLL, n_chunks)

    def trip(j, ms, last):
        c0 = unroll * j
        for u in range(unroll):
            if last and u == unroll - 1:
                ms = half(0, qn_ref, c0 + u, u % 2, ms)
            else:
                ms = half(c0 + u + 1, q_ref, c0 + u, u % 2, ms)
        return ms

    m0 = tuple(jnp.full((1, tq), NEG_BIG, F32) for _ in heads)
    n_trips = n_chunks // unroll
    ms = lax.fori_loop(0, n_trips - 1, functools.partial(trip, last=False), m0)
    trip(n_trips - 1, ms, last=True)
    for g in heads:
        o = acc_ref[0:HEAD_DIM, strip(g)] / acc_ref[HEAD_DIM:HEAD_DIM + 1, strip(g)]
        o_ref[:, g * HEAD_DIM:(g + 1) * HEAD_DIM] = o.T.astype(o_ref.dtype)


def _dense_attention(qb, kb, vbt, tq=DENSE_TQ):
    b, s, dq = qb.shape
    tk = vbt.shape[-1]
    assert math.gcd(DENSE_UNROLL, s // tk) % 2 == 0, "the two score slots alternate per chunk"
    gw = GQA_GROUP * HEAD_DIM
    n_tiles = s // tq
    return pl.pallas_call(
        functools.partial(_dense_kernel, tq=tq, tk=tk),
        grid=(b, N_KV_B, n_tiles),
        in_specs=[pl.BlockSpec((None, tq, gw), lambda bi, h, i: (bi, i, h)),
                  pl.BlockSpec((None, tq, gw),
                               lambda bi, h, i: (bi, jnp.minimum(i + 1, n_tiles - 1), h)),
                  pl.BlockSpec((None, None, s, HEAD_DIM), lambda bi, h, i: (bi, h, 0, 0)),
                  pl.BlockSpec((None, None, s // tk, HEAD_DIM, tk),
                               lambda bi, h, i: (bi, h, 0, 0, 0))],
        out_specs=pl.BlockSpec((None, tq, gw), lambda bi, h, i: (bi, i, h)),
        out_shape=jax.ShapeDtypeStruct((b, s, dq), BF16),
        scratch_shapes=[pltpu.VMEM((HEAD_DIM + SUM_ROWS, GQA_GROUP * tq), F32),
                        pltpu.VMEM((2, tk, GQA_GROUP * tq), F32),
                        pltpu.VMEM((2, 1, GQA_GROUP * tq), F32)],
        compiler_params=_params(("arbitrary", "arbitrary", "arbitrary")),
        name="dense_gqa_attention",
    )(qb, qb, kb, vbt)


def _merge_kernel(*refs, tm, rc, ng):
    o_refs, lse_refs = refs[0:ng], refs[ng:2 * ng]
    yb_ref, wa_ref, wb_ref, g_ref, out_ref, ya_ref = refs[2 * ng:]
    d = out_ref.shape[-1]

    for h in range(HEADS_PER_GROUP_A):
        lses = [lse_refs[g][h] for g in range(ng)]
        top = functools.reduce(jnp.maximum, lses)
        es = [jnp.exp2(x - top) for x in lses]
        num = sum(e * o_refs[g][h].astype(F32) for g, e in enumerate(es))
        ya_ref[:, h * HEAD_DIM:(h + 1) * HEAD_DIM] = (num / sum(es)).astype(ya_ref.dtype)

    for r in range(tm // rc):
        r0, r1 = r * rc, (r + 1) * rc
        pa = jnp.dot(ya_ref[r0:r1, :], wa_ref[...], preferred_element_type=F32)
        pb = jnp.dot(yb_ref[r0:r1, :], wb_ref[...], preferred_element_type=F32)
        sa = g_ref[r0:r1, 0:d].astype(F32)
        sb = g_ref[r0:r1, d:2 * d].astype(F32)
        out_ref[r0:r1, :] = (sa * pa + sb * pb).astype(out_ref.dtype)


def _gated_merge(group_outs, group_lses, yb, wa, wb, gates, tm=512, rc=256):
    ng = len(group_outs)
    b, hpg, s, _ = group_outs[0].shape
    da = hpg * HEAD_DIM
    db = yb.shape[-1]
    d = wa.shape[1]
    hblk = pl.BlockSpec((None, hpg, tm, HEAD_DIM), lambda bi, i: (bi, 0, i, 0))
    once = pl.Buffered(1)
    return pl.pallas_call(
        functools.partial(_merge_kernel, tm=tm, rc=rc, ng=ng),
        grid=(b, s // tm),
        in_specs=[hblk] * (2 * ng) + [
            pl.BlockSpec((None, tm, db), lambda bi, i: (bi, i, 0)),
            pl.BlockSpec((da, d), lambda bi, i: (0, 0), pipeline_mode=once),
            pl.BlockSpec((db, d), lambda bi, i: (0, 0), pipeline_mode=once),
            pl.BlockSpec((None, tm, 2 * d), lambda bi, i: (bi, i, 0))],
        out_specs=pl.BlockSpec((None, tm, d), lambda bi, i: (bi, i, 0)),
        out_shape=jax.ShapeDtypeStruct((b, s, d), BF16),
        scratch_shapes=[pltpu.VMEM((tm, da), BF16)],
        compiler_params=_params(("parallel", "parallel")),
        name="gated_merge",
    )(*group_outs, *group_lses, yb, wa, wb, gates)


def _oproj_kernel(m_ref, w_ref, x_ref, o_ref, *, tm, rc):
    for r in range(tm // rc):
        r0, r1 = r * rc, (r + 1) * rc
        o_ref[r0:r1, :] = x_ref[r0:r1, :] + jnp.dot(m_ref[r0:r1, :], w_ref[...],
                                                    preferred_element_type=F32)


def _oproj_residual(merged, w_o, x, tm=1024, tn=1024, rc=256):
    b, s, d = merged.shape
    return pl.pallas_call(
        functools.partial(_oproj_kernel, tm=tm, rc=rc),
        grid=(b, s // tm, d // tn),
        in_specs=[pl.BlockSpec((None, tm, d), lambda bi, i, j: (bi, i, 0)),
                  pl.BlockSpec((d, tn), lambda bi, i, j: (0, j)),
                  pl.BlockSpec((None, tm, tn), lambda bi, i, j: (bi, i, j))],
        out_specs=pl.BlockSpec((None, tm, tn), lambda bi, i, j: (bi, i, j)),
        out_shape=jax.ShapeDtypeStruct((b, s, d), F32),
        compiler_params=_params(("parallel", "parallel", "arbitrary")),
        name="oproj_residual",
    )(merged, w_o, x)


def _ffn_kernel(x_ref, gffn_ref, gfin_ref, wg_ref, wu_ref, wd_ref, o_ref, h_ref, *, tm, rc):
    f = pl.program_id(2)

    @pl.when(f == 0)
    def _():
        x = x_ref[...]
        ms = jnp.mean(x * x, axis=-1, keepdims=True)
        h_ref[...] = (x * lax.rsqrt(ms + EPS) * gffn_ref[...]).astype(h_ref.dtype)
        o_ref[...] = jnp.zeros(o_ref.shape, F32)

    for r in range(tm // rc):
        r0, r1 = r * rc, (r + 1) * rc
        h = h_ref[r0:r1, :]
        gate = jnp.dot(h, wg_ref[...], preferred_element_type=F32)
        up = jnp.dot(h, wu_ref[...], preferred_element_type=F32)
        act = (gate / (1.0 + jnp.exp(-gate))) * up
        o_ref[r0:r1, :] += jnp.dot(act.astype(BF16), wd_ref[...], preferred_element_type=F32)

    @pl.when(f == pl.num_programs(2) - 1)
    def _():
        y = x_ref[...] + o_ref[...]
        ms = jnp.mean(y * y, axis=-1, keepdims=True)
        o_ref[...] = y * lax.rsqrt(ms + EPS) * gfin_ref[...]


def _ffn_final(x, g_ffn, g_final, w_gu, w_down, tm=1024, tf=512, rc=512):
    b, s, d = x.shape
    dff = w_down.shape[0]
    nf = dff // tf
    return pl.pallas_call(
        functools.partial(_ffn_kernel, tm=tm, rc=rc),
        grid=(b, s // tm, nf),
        in_specs=[pl.BlockSpec((None, tm, d), lambda bi, i, f: (bi, i, 0)),
                  pl.BlockSpec((1, d), lambda bi, i, f: (0, 0)),
                  pl.BlockSpec((1, d), lambda bi, i, f: (0, 0)),
                  pl.BlockSpec((d, tf), lambda bi, i, f: (0, f)),
                  pl.BlockSpec((d, tf), lambda bi, i, f: (0, nf + f)),
                  pl.BlockSpec((tf, d), lambda bi, i, f: (f, 0))],
        out_specs=pl.BlockSpec((None, tm, d), lambda bi, i, f: (bi, i, 0)),
        out_shape=jax.ShapeDtypeStruct((b, s, d), F32),
        scratch_shapes=[pltpu.VMEM((tm, d), BF16)],
        compiler_params=_params(("parallel", "parallel", "arbitrary"), vmem_limit=FFN_VMEM_LIMIT),
        name="ffn_final",
    )(x, g_ffn.reshape(1, d).astype(F32), g_final.reshape(1, d).astype(F32), w_gu, w_gu, w_down)


def _rope_tables(s):
    half = HEAD_DIM // 2
    inv = ROPE_THETA ** (-(jnp.arange(half, dtype=F32) * 2.0 / HEAD_DIM))
    ang = jnp.arange(s).astype(F32)[:, None] * inv[None, :]
    cos, sin = jnp.cos(ang), jnp.sin(ang)
    return jnp.concatenate([cos, cos], axis=-1), jnp.concatenate([-sin, sin], axis=-1)


def _axial_tables(s):
    dim = HEAD_DIM // 2
    half = dim // 2
    inv = ROPE_THETA ** (-(jnp.arange(half, dtype=F32) * 2.0 / dim))
    n_rows = s // GRID_W
    row = jnp.repeat(jnp.arange(n_rows), GRID_W).astype(F32)
    col = jnp.tile(jnp.arange(GRID_W), n_rows).astype(F32)
    cs, sn = [], []
    for pos in (row, col):
        ang = pos[:, None] * inv[None, :]
        c, sgn = jnp.cos(ang), jnp.sin(ang)
        cs += [c, c]
        sn += [-sgn, sgn]
    return jnp.concatenate(cs, axis=-1), jnp.concatenate(sn, axis=-1)


def _trunk(x, p):
    b, s, d = x.shape
    a_w = N_HEADS_A * HEAD_DIM
    b_q = d
    b_kv = N_KV_B * HEAD_DIM
    cuts = np.cumsum([0, a_w, a_w, a_w, b_q, b_kv, b_kv, 2 * d]).tolist()
    w_in = p["w_in"]

    def wcols(k):
        return w_in, cuts[k], cuts[k + 1] - cuts[k]

    h = _rmsnorm_bf16(x, p["g_attn"])
    cos1, sin1 = p["rope"]
    cos2, sin2 = p["axial"]

    gw = HEADS_PER_GROUP_A * HEAD_DIM
    qkv_a = []
    for col0, n, dt, tag in ((0, gw, BF16, "g0"), (gw, a_w - gw, F32, "g12")):
        qkv_a.append((
            _proj(h, w_in, cuts[0] + col0, n, kind="rope", out_kind="heads", out_dtype=dt,
                  tables=(cos1 * QSCALE, sin1 * QSCALE), name="proj_qa_" + tag),
            _proj(h, w_in, cuts[1] + col0, n, kind="rope", out_kind="heads", out_dtype=dt,
                  tables=(cos1, sin1), name="proj_ka_" + tag),
            _proj(h, w_in, cuts[2] + col0, n, kind="plain", out_kind="heads", out_dtype=dt,
                  name="proj_va_" + tag)))
    qb = _proj(h, *wcols(3), kind="qknorm", out_kind="natural", tables=(cos2, sin2),
               gain=p["q_gain"] * QSCALE, name="proj_qb")
    kb = _proj(h, *wcols(4), kind="qknorm", out_kind="heads", tables=(cos2, sin2),
               gain=p["k_gain"], name="proj_kb")
    vbt = _proj(h, *wcols(5), kind="plain", out_kind="vT", rc=DENSE_TK, name="proj_vb")
    gates = _proj(h, *wcols(6), kind="sigmoid", out_kind="natural", name="proj_gates")

    groups = [_banded_group(*qkv_a[0], 0, 0)]
    groups += [_banded_group(*qkv_a[1], g, (g - 1) * HEADS_PER_GROUP_A)
               for g in range(1, len(DILATIONS))]
    yb = _dense_attention(qb, kb, vbt)
    merged = _gated_merge([o for o, _ in groups], [l for _, l in groups], yb,
                          p["w_a_br"], p["w_b_br"], gates)
    x1 = _oproj_residual(merged, p["w_o"], x)
    return _ffn_final(x1, p["g_ffn"], p["g_final"], p["w_gate_up"], p["w_down"])


def kernel(x_prompt, x_sample, g_attn, w_in, q_gain_b, k_gain_b, w_a_br, w_b_br, w_o, g_ffn,
           w_gate_up, w_down, g_final):
    assert g_attn.shape[0] == 1, "single-layer trunk"
    s = x_prompt.shape[1]
    assert x_sample.shape[1] == s and s % Q_CHUNK == 0
    p = {
        "g_attn": g_attn[0], "g_ffn": g_ffn[0], "g_final": g_final,
        "w_in": w_in[0].astype(BF16),
        "q_gain": q_gain_b[0].reshape(1, HEAD_DIM).astype(F32),
        "k_gain": k_gain_b[0].reshape(1, HEAD_DIM).astype(F32),
        "w_a_br": w_a_br[0].astype(BF16), "w_b_br": w_b_br[0].astype(BF16),
        "w_o": w_o[0].astype(BF16),
        "w_gate_up": w_gate_up[0].astype(BF16), "w_down": w_down[0].astype(BF16),
        "rope": _rope_tables(s), "axial": _axial_tables(s),
    }
    return (_trunk(x_prompt, p), _trunk(x_sample, p))
```

```python
import functools
import math

import numpy as np
import jax
import jax.numpy as jnp
from jax import lax
from jax.experimental import pallas as pl
from jax.experimental.pallas import tpu as pltpu

HEAD_DIM = 128
DILATIONS = (1, 4, 16)
BAND_RADIUS = 64
HEADS_PER_GROUP_A = 8
N_HEADS_A = HEADS_PER_GROUP_A * len(DILATIONS)
N_KV_B = 4
GQA_GROUP = 4
GRID_W = 64
ROPE_THETA = 10000.0
EPS = 1e-6
NEG_BIG = -1e30

VMEM_BYTES_V7X = 64 * 1024 * 1024
VMEM_LIMIT = VMEM_BYTES_V7X - 8 * 1024 * 1024
FFN_VMEM_LIMIT = VMEM_BYTES_V7X - 3 * 1024 * 1024

F32 = jnp.float32
BF16 = jnp.bfloat16

QSCALE = (HEAD_DIM ** -0.5) * math.log2(math.e)

Q_CHUNK = 4 * BAND_RADIUS * max(DILATIONS)
BAND_TILE = 128
SPLIT_STRIDE = 4

DENSE_TQ = 256
DENSE_TK = 1024
DENSE_UNROLL = 8
SUM_ROWS = 16


def _params(sem, vmem_limit=VMEM_LIMIT):
    return pltpu.CompilerParams(dimension_semantics=sem, vmem_limit_bytes=vmem_limit)


def _rmsnorm_kernel(x_ref, g_ref, o_ref):
    x = x_ref[...]
    ms = jnp.mean(x * x, axis=-1, keepdims=True)
    o_ref[...] = (x * lax.rsqrt(ms + EPS) * g_ref[...]).astype(o_ref.dtype)


def _rmsnorm_bf16(x, g, tm=512):
    b, s, d = x.shape
    return pl.pallas_call(
        _rmsnorm_kernel,
        grid=(b, s // tm),
        in_specs=[pl.BlockSpec((None, tm, d), lambda bi, i: (bi, i, 0)),
                  pl.BlockSpec((1, d), lambda bi, i: (0, 0))],
        out_specs=pl.BlockSpec((None, tm, d), lambda bi, i: (bi, i, 0)),
        out_shape=jax.ShapeDtypeStruct((b, s, d), BF16),
        compiler_params=_params(("parallel", "parallel")),
        name="rmsnorm_bf16",
    )(x, g.reshape(1, d).astype(F32))


def _proj_kernel(*refs, kind, out_kind, nh, rc, tm):
    h_ref, w_ref = refs[0], refs[1]
    o_ref = refs[-1]
    if kind in ("rope", "qknorm"):
        cos_ref, sin_ref = refs[2], refs[3]
    if kind == "qknorm":
        gain_ref = refs[4]
        lane = lax.broadcasted_iota(jnp.int32, (rc, HEAD_DIM), 1)
        lo = (lane % 64) < 32

    def epilogue(x, c, s):
        if kind == "rope":
            return x * c + pltpu.roll(x, 64, 1) * s
        if kind == "qknorm":
            ms = jnp.mean(x * x, axis=-1, keepdims=True)
            y = x * lax.rsqrt(ms + EPS) * gain_ref[...]
            yr = jnp.where(lo, pltpu.roll(y, 96, 1), pltpu.roll(y, 32, 1))
            return y * c + yr * s
        if kind == "sigmoid":
            return 1.0 / (1.0 + jnp.exp(-x))
        return x

    for r in range(tm // rc):
        r0, r1 = r * rc, (r + 1) * rc
        acc = jnp.dot(h_ref[r0:r1, :], w_ref[...], preferred_element_type=F32)
        c = s = None
        if kind in ("rope", "qknorm"):
            c, s = cos_ref[r0:r1, :], sin_ref[r0:r1, :]
        for j in range(nh):
            y = epilogue(acc[:, j * HEAD_DIM:(j + 1) * HEAD_DIM], c, s)
            if out_kind == "heads":
                o_ref[j, r0:r1, :] = y.astype(o_ref.dtype)
            elif out_kind == "natural":
                o_ref[r0:r1, j * HEAD_DIM:(j + 1) * HEAD_DIM] = y.astype(o_ref.dtype)
            else:
                o_ref[j, r] = y.T.astype(o_ref.dtype)


def _proj(h, w, col0, n, *, kind, out_kind, out_dtype=BF16, tables=(), gain=None,
          tm=2048, tn=1024, rc=256, name):
    b, s, d = h.shape
    tn = min(tn, n)
    assert col0 % tn == 0 and n % tn == 0 and s % tm == 0
    cb0 = col0 // tn
    nh = tn // HEAD_DIM

    in_specs = [pl.BlockSpec((None, tm, d), lambda bi, i, j: (bi, i, 0)),
                pl.BlockSpec((d, tn), lambda bi, i, j: (0, cb0 + j))]
    args = [h, w]
    for t in tables:
        in_specs.append(pl.BlockSpec((tm, HEAD_DIM), lambda bi, i, j: (i, 0)))
        args.append(t)
    if gain is not None:
        in_specs.append(pl.BlockSpec((1, HEAD_DIM), lambda bi, i, j: (0, 0)))
        args.append(gain)

    if out_kind == "heads":
        out_shape = (b, n // HEAD_DIM, s, HEAD_DIM)
        out_spec = pl.BlockSpec((None, nh, tm, HEAD_DIM), lambda bi, i, j: (bi, j, i, 0))
    elif out_kind == "natural":
        out_shape = (b, s, n)
        out_spec = pl.BlockSpec((None, tm, tn), lambda bi, i, j: (bi, i, j))
    else:
        out_shape = (b, n // HEAD_DIM, s // rc, HEAD_DIM, rc)
        out_spec = pl.BlockSpec((None, nh, tm // rc, HEAD_DIM, rc),
                                lambda bi, i, j: (bi, j, i, 0, 0))

    kern = functools.partial(_proj_kernel, kind=kind, out_kind=out_kind, nh=nh, rc=rc, tm=tm)
    return pl.pallas_call(
        kern,
        grid=(b, s // tm, n // tn),
        in_specs=in_specs,
        out_specs=out_spec,
        out_shape=jax.ShapeDtypeStruct(out_shape, out_dtype),
        compiler_params=_params(("parallel", "parallel", "arbitrary")),
        name=name,
    )(*args)


def _band_bias(t):
    qi = np.arange(t)[:, None]
    kj = np.arange(t + 2 * BAND_RADIUS)[None, :]
    ok = np.abs(kj - BAND_RADIUS - qi) <= BAND_RADIUS
    return np.where(ok, 0.0, NEG_BIG).astype(np.float32)


def _band_kernel(q_ref, kp_ref, kc_ref, kn_ref, vp_ref, vc_ref, vn_ref, bias_ref,
                 o_ref, lse_ref, *scratch, d, t, cm, seq_m):
    c = pl.program_id(2)
    rad = BAND_RADIUS
    win = t + 2 * rad
    operands = (q_ref, kp_ref, kc_ref, kn_ref, vp_ref, vc_ref, vn_ref)
    stages = scratch[2:]

    two_pass = d > SPLIT_STRIDE
    if two_pass:
        for ref, stage in zip(operands, stages):
            part = ref.shape[0] // SPLIT_STRIDE
            for r4 in range(SPLIT_STRIDE):
                stage[r4 * part:(r4 + 1) * part, :] = ref[pl.ds(r4, part, stride=SPLIT_STRIDE), :]

    def rows(i, r, n):
        ref = operands[i]
        if d == 1:
            return ref[...]
        if not two_pass:
            return ref[pl.ds(r, n, stride=d), :].astype(BF16)
        part = ref.shape[0] // SPLIT_STRIDE
        start = (r % SPLIT_STRIDE) * part + r // SPLIT_STRIDE
        return stages[i][pl.ds(start, n, stride=d // SPLIT_STRIDE), :].astype(BF16)

    biases = []
    for u in range(cm // t):
        kpos = c * cm + u * t - rad + lax.broadcasted_iota(jnp.int32, (1, win), 1)
        kvalid = jnp.where((kpos >= 0) & (kpos < seq_m), 0.0, NEG_BIG)
        biases.append(bias_ref[...] + kvalid)

    for r in range(d):
        q = rows(0, r, cm)
        kw = jnp.concatenate([rows(1, r, rad), rows(2, r, cm), rows(3, r, rad)], axis=0)
        vw = jnp.concatenate([rows(4, r, rad), rows(5, r, cm), rows(6, r, rad)], axis=0)
        for u in range(cm // t):
            m0 = u * t
            s = lax.dot_general(q[m0:m0 + t], kw[m0:m0 + win], (((1,), (1,)), ((), ())),
                                preferred_element_type=F32) + biases[u]
            m = jnp.max(s, axis=-1, keepdims=True)
            p = jnp.exp2(s - m)
            l = jnp.sum(p, axis=-1, keepdims=True)
            o = jnp.dot(p.astype(BF16), vw[m0:m0 + win], preferred_element_type=F32) / l
            lse = jnp.broadcast_to(m + jnp.log2(l), (t, HEAD_DIM))
            if d == 1:
                o_ref[m0:m0 + t, :] = o.astype(o_ref.dtype)
                lse_ref[m0:m0 + t, :] = lse
            else:
                o_scr, lse_scr = scratch[0], scratch[1]
                o_scr[pl.ds(r + d * m0, t, stride=d), :] = o
                lse_scr[pl.ds(r + d * m0, t, stride=d), :] = lse
    if d > 1:
        o_ref[...] = scratch[0][...].astype(o_ref.dtype)
        lse_ref[...] = scratch[1][...]


def _banded_group(q, k, v, g, head0):
    b, _, s, _ = q.shape
    d = DILATIONS[g]
    hpg = HEADS_PER_GROUP_A
    cm = Q_CHUNK // d
    t = min(BAND_TILE, cm)
    halo = BAND_RADIUS * d
    per = Q_CHUNK // halo
    last = s // halo - 1
    blk = pl.BlockSpec((None, None, Q_CHUNK, HEAD_DIM), lambda bi, hg, c: (bi, head0 + hg, c, 0))
    prev = pl.BlockSpec((None, None, halo, HEAD_DIM),
                        lambda bi, hg, c: (bi, head0 + hg, jnp.maximum(c * per - 1, 0), 0))
    nxt = pl.BlockSpec((None, None, halo, HEAD_DIM),
                       lambda bi, hg, c: (bi, head0 + hg, jnp.minimum((c + 1) * per, last), 0))
    out = pl.BlockSpec((None, None, Q_CHUNK, HEAD_DIM), lambda bi, hg, c: (bi, hg, c, 0))
    scratch = [] if d == 1 else [pltpu.VMEM((Q_CHUNK, HEAD_DIM), F32)] * 2
    if d > SPLIT_STRIDE:
        scratch += [pltpu.VMEM((n, HEAD_DIM), F32) for n in (Q_CHUNK,) + (halo, Q_CHUNK, halo) * 2]
    return pl.pallas_call(
        functools.partial(_band_kernel, d=d, t=t, cm=cm, seq_m=s // d),
        grid=(b, hpg, s // Q_CHUNK),
        in_specs=[blk, prev, blk, nxt, prev, blk, nxt,
                  pl.BlockSpec((t, t + 2 * BAND_RADIUS), lambda bi, hg, c: (0, 0))],
        out_specs=[out, out],
        out_shape=[jax.ShapeDtypeStruct((b, hpg, s, HEAD_DIM), BF16),
                   jax.ShapeDtypeStruct((b, hpg, s, HEAD_DIM), F32)],
        scratch_shapes=scratch,
        compiler_params=_params(("parallel", "parallel", "arbitrary")),
        name=f"banded_attention_d{d}",
    )(q, k, k, k, v, v, v, jnp.asarray(_band_bias(t)))


def _dense_kernel(q_ref, qn_ref, k_ref, vt_ref, o_ref, acc_ref, s_ref, mx_ref, *, tq, tk):
    n_chunks = k_ref.shape[0] // tk
    heads = range(GQA_GROUP)

    def strip(g):
        return slice(g * tq, (g + 1) * tq)

    acc_ref[...] = jnp.zeros(acc_ref.shape, F32)

    def scores(c, slot, g, ref=q_ref):
        k0 = pl.multiple_of(c * tk, tk)
        q = ref[:, g * HEAD_DIM:(g + 1) * HEAD_DIM]
        st = lax.dot_general(k_ref[pl.ds(k0, tk), :], q, (((1,), (1,)), ((), ())),
                             preferred_element_type=F32)
        s_ref[slot, :, strip(g)] = st
        mx_ref[slot, :, strip(g)] = jnp.max(st, axis=0, keepdims=True)

    ones_rows = jnp.ones((SUM_ROWS, tk), BF16)

    def update(c, slot, g, m):
        m_new = jnp.maximum(m, mx_ref[slot, :, strip(g)])
        alpha = jnp.exp2(m - m_new)
        p = jnp.exp2(s_ref[slot, :, strip(g)] - m_new).astype(BF16)
        vt1 = jnp.concatenate([vt_ref[c], ones_rows], axis=0)
        pv = jnp.dot(vt1, p, preferred_element_type=F32)
        acc_ref[:, strip(g)] = acc_ref[:, strip(g)] * alpha + pv
        return m_new

    @pl.when(pl.program_id(2) == 0)
    def _():
        for g in heads:
            scores(0, 0, g)

    def half(c_next, next_ref, c, slot, ms):
        out = []
        for g in heads:
            scores(c_next, 1 - slot, g, next_ref)
            out.append(update(c, slot, g, ms[g]))
        return tuple(out)

    unroll = math.gcd(DENSE_UNROLL, n_chunks)

    def trip(j, ms, last):
        c0 = unroll * j
        for u in range(unroll):
            if last and u == unroll - 1:
                ms = half(0, qn_ref, c0 + u, u % 2, ms)
            else:
                ms = half(c0 + u + 1, q_ref, c0 + u, u % 2, ms)
        return ms

    m0 = tuple(jnp.full((1, tq), NEG_BIG, F32) for _ in heads)
    n_trips = n_chunks // unroll
    ms = lax.fori_loop(0, n_trips - 1, functools.partial(trip, last=False), m0)
    trip(n_trips - 1, ms, last=True)
    for g in heads:
        o = acc_ref[0:HEAD_DIM, strip(g)] / acc_ref[HEAD_DIM:HEAD_DIM + 1, strip(g)]
        o_ref[:, g * HEAD_DIM:(g + 1) * HEAD_DIM] = o.T.astype(o_ref.dtype)


def _dense_attention(qb, kb, vbt, tq=DENSE_TQ):
    b, s, dq = qb.shape
    tk = vbt.shape[-1]
    assert math.gcd(DENSE_UNROLL, s // tk) % 2 == 0, "the two score slots alternate per chunk"
    gw = GQA_GROUP * HEAD_DIM
    n_tiles = s // tq
    return pl.pallas_call(
        functools.partial(_dense_kernel, tq=tq, tk=tk),
        grid=(b, N_KV_B, n_tiles),
        in_specs=[pl.BlockSpec((None, tq, gw), lambda bi, h, i: (bi, i, h)),
                  pl.BlockSpec((None, tq, gw),
                               lambda bi, h, i: (bi, jnp.minimum(i + 1, n_tiles - 1), h)),
                  pl.BlockSpec((None, None, s, HEAD_DIM), lambda bi, h, i: (bi, h, 0, 0)),
                  pl.BlockSpec((None, None, s // tk, HEAD_DIM, tk),
                               lambda bi, h, i: (bi, h, 0, 0, 0))],
        out_specs=pl.BlockSpec((None, tq, gw), lambda bi, h, i: (bi, i, h)),
        out_shape=jax.ShapeDtypeStruct((b, s, dq), BF16),
        scratch_shapes=[pltpu.VMEM((HEAD_DIM + SUM_ROWS, GQA_GROUP * tq), F32),
                        pltpu.VMEM((2, tk, GQA_GROUP * tq), F32),
                        pltpu.VMEM((2, 1, GQA_GROUP * tq), F32)],
        compiler_params=_params(("arbitrary", "arbitrary", "arbitrary")),
        name="dense_gqa_attention",
    )(qb, qb, kb, vbt)


def _merge_kernel(*refs, tm, rc, ng):
    o_refs, lse_refs = refs[0:ng], refs[ng:2 * ng]
    yb_ref, wa_ref, wb_ref, g_ref, out_ref, ya_ref = refs[2 * ng:]
    d = out_ref.shape[-1]

    for h in range(HEADS_PER_GROUP_A):
        lses = [lse_refs[g][h] for g in range(ng)]
        top = functools.reduce(jnp.maximum, lses)
        es = [jnp.exp2(x - top) for x in lses]
        num = sum(e * o_refs[g][h].astype(F32) for g, e in enumerate(es))
        ya_ref[:, h * HEAD_DIM:(h + 1) * HEAD_DIM] = (num / sum(es)).astype(ya_ref.dtype)

    for r in range(tm // rc):
        r0, r1 = r * rc, (r + 1) * rc
        pa = jnp.dot(ya_ref[r0:r1, :], wa_ref[...], preferred_element_type=F32)
        pb = jnp.dot(yb_ref[r0:r1, :], wb_ref[...], preferred_element_type=F32)
        sa = g_ref[r0:r1, 0:d].astype(F32)
        sb = g_ref[r0:r1, d:2 * d].astype(F32)
        out_ref[r0:r1, :] = (sa * pa + sb * pb).astype(out_ref.dtype)


def _gated_merge(group_outs, group_lses, yb, wa, wb, gates, tm=512, rc=256):
    ng = len(group_outs)
    b, hpg, s, _ = group_outs[0].shape
    da = hpg * HEAD_DIM
    db = yb.shape[-1]
    d = wa.shape[1]
    hblk = pl.BlockSpec((None, hpg, tm, HEAD_DIM), lambda bi, i: (bi, 0, i, 0))
    once = pl.Buffered(1)
    return pl.pallas_call(
        functools.partial(_merge_kernel, tm=tm, rc=rc, ng=ng),
        grid=(b, s // tm),
        in_specs=[hblk] * (2 * ng) + [
            pl.BlockSpec((None, tm, db), lambda bi, i: (bi, i, 0)),
            pl.BlockSpec((da, d), lambda bi, i: (0, 0), pipeline_mode=once),
            pl.BlockSpec((db, d), lambda bi, i: (0, 0), pipeline_mode=once),
            pl.BlockSpec((None, tm, 2 * d), lambda bi, i: (bi, i, 0))],
        out_specs=pl.BlockSpec((None, tm, d), lambda bi, i: (bi, i, 0)),
        out_shape=jax.ShapeDtypeStruct((b, s, d), BF16),
        scratch_shapes=[pltpu.VMEM((tm, da), BF16)],
        compiler_params=_params(("parallel", "parallel")),
        name="gated_merge",
    )(*group_outs, *group_lses, yb, wa, wb, gates)


def _oproj_kernel(m_ref, w_ref, x_ref, o_ref, *, tm, rc):
    for r in range(tm // rc):
        r0, r1 = r * rc, (r + 1) * rc
        o_ref[r0:r1, :] = x_ref[r0:r1, :] + jnp.dot(m_ref[r0:r1, :], w_ref[...],
                                                    preferred_element_type=F32)


def _oproj_residual(merged, w_o, x, tm=1024, rc=256):
    b, s, d = merged.shape
    return pl.pallas_call(
        functools.partial(_oproj_kernel, tm=tm, rc=rc),
        grid=(b, s // tm),
        in_specs=[pl.BlockSpec((None, tm, d), lambda bi, i: (bi, i, 0)),
                  pl.BlockSpec((d, d), lambda bi, i: (0, 0), pipeline_mode=pl.Buffered(1)),
                  pl.BlockSpec((None, tm, d), lambda bi, i: (bi, i, 0))],
        out_specs=pl.BlockSpec((None, tm, d), lambda bi, i: (bi, i, 0)),
        out_shape=jax.ShapeDtypeStruct((b, s, d), F32),
        compiler_params=_params(("parallel", "parallel")),
        name="oproj_residual",
    )(merged, w_o, x)


def _ffn_kernel(x_ref, gffn_ref, gfin_ref, wg_ref, wu_ref, wd_ref, o_ref, h_ref, *, tm, rc):
    f = pl.program_id(2)

    @pl.when(f == 0)
    def _():
        x = x_ref[...]
        ms = jnp.mean(x * x, axis=-1, keepdims=True)
        h_ref[...] = (x * lax.rsqrt(ms + EPS) * gffn_ref[...]).astype(h_ref.dtype)
        o_ref[...] = jnp.zeros(o_ref.shape, F32)

    for r in range(tm // rc):
        r0, r1 = r * rc, (r + 1) * rc
        h = h_ref[r0:r1, :]
        gate = jnp.dot(h, wg_ref[...], preferred_element_type=F32)
        up = jnp.dot(h, wu_ref[...], preferred_element_type=F32)
        act = (gate / (1.0 + jnp.exp(-gate))) * up
        o_ref[r0:r1, :] += jnp.dot(act.astype(BF16), wd_ref[...], preferred_element_type=F32)

    @pl.when(f == pl.num_programs(2) - 1)
    def _():
        y = x_ref[...] + o_ref[...]
        ms = jnp.mean(y * y, axis=-1, keepdims=True)
        o_ref[...] = y * lax.rsqrt(ms + EPS) * gfin_ref[...]


def _ffn_final(x, g_ffn, g_final, w_gu, w_down, tm=1024, tf=512, rc=512):
    b, s, d = x.shape
    dff = w_down.shape[0]
    nf = dff // tf
    return pl.pallas_call(
        functools.partial(_ffn_kernel, tm=tm, rc=rc),
        grid=(b, s // tm, nf),
        in_specs=[pl.BlockSpec((None, tm, d), lambda bi, i, f: (bi, i, 0)),
                  pl.BlockSpec((1, d), lambda bi, i, f: (0, 0)),
                  pl.BlockSpec((1, d), lambda bi, i, f: (0, 0)),
                  pl.BlockSpec((d, tf), lambda bi, i, f: (0, f)),
                  pl.BlockSpec((d, tf), lambda bi, i, f: (0, nf + f)),
                  pl.BlockSpec((tf, d), lambda bi, i, f: (f, 0))],
        out_specs=pl.BlockSpec((None, tm, d), lambda bi, i, f: (bi, i, 0)),
        out_shape=jax.ShapeDtypeStruct((b, s, d), F32),
        scratch_shapes=[pltpu.VMEM((tm, d), BF16)],
        compiler_params=_params(("parallel", "parallel", "arbitrary"), vmem_limit=FFN_VMEM_LIMIT),
        name="ffn_final",
    )(x, g_ffn.reshape(1, d).astype(F32), g_final.reshape(1, d).astype(F32), w_gu, w_gu, w_down)


def _rope_tables(s):
    half = HEAD_DIM // 2
    inv = ROPE_THETA ** (-(jnp.arange(half, dtype=F32) * 2.0 / HEAD_DIM))
    ang = jnp.arange(s).astype(F32)[:, None] * inv[None, :]
    cos, sin = jnp.cos(ang), jnp.sin(ang)
    return jnp.concatenate([cos, cos], axis=-1), jnp.concatenate([-sin, sin], axis=-1)


def _axial_tables(s):
    dim = HEAD_DIM // 2
    half = dim // 2
    inv = ROPE_THETA ** (-(jnp.arange(half, dtype=F32) * 2.0 / dim))
    n_rows = s // GRID_W
    row = jnp.repeat(jnp.arange(n_rows), GRID_W).astype(F32)
    col = jnp.tile(jnp.arange(GRID_W), n_rows).astype(F32)
    cs, sn = [], []
    for pos in (row, col):
        ang = pos[:, None] * inv[None, :]
        c, sgn = jnp.cos(ang), jnp.sin(ang)
        cs += [c, c]
        sn += [-sgn, sgn]
    return jnp.concatenate(cs, axis=-1), jnp.concatenate(sn, axis=-1)


def _trunk(x, p):
    b, s, d = x.shape
    a_w = N_HEADS_A * HEAD_DIM
    b_q = d
    b_kv = N_KV_B * HEAD_DIM
    cuts = np.cumsum([0, a_w, a_w, a_w, b_q, b_kv, b_kv, 2 * d]).tolist()
    w_in = p["w_in"]

    def wcols(k):
        return w_in, cuts[k], cuts[k + 1] - cuts[k]

    h = _rmsnorm_bf16(x, p["g_attn"])
    cos1, sin1 = p["rope"]
    cos2, sin2 = p["axial"]

    gw = HEADS_PER_GROUP_A * HEAD_DIM
    qkv_a = []
    for col0, n, dt, tag in ((0, gw, BF16, "g0"), (gw, a_w - gw, F32, "g12")):
        qkv_a.append((
            _proj(h, w_in, cuts[0] + col0, n, kind="rope", out_kind="heads", out_dtype=dt,
                  tables=(cos1 * QSCALE, sin1 * QSCALE), name="proj_qa_" + tag),
            _proj(h, w_in, cuts[1] + col0, n, kind="rope", out_kind="heads", out_dtype=dt,
                  tables=(cos1, sin1), name="proj_ka_" + tag),
            _proj(h, w_in, cuts[2] + col0, n, kind="plain", out_kind="heads", out_dtype=dt,
                  name="proj_va_" + tag)))
    qb = _proj(h, *wcols(3), kind="qknorm", out_kind="natural", tables=(cos2, sin2),
               gain=p["q_gain"] * QSCALE, name="proj_qb")
    kb = _proj(h, *wcols(4), kind="qknorm", out_kind="heads", tables=(cos2, sin2),
               gain=p["k_gain"], name="proj_kb")
    vbt = _proj(h, *wcols(5), kind="plain", out_kind="vT", rc=DENSE_TK, name="proj_vb")
    gates = _proj(h, *wcols(6), kind="sigmoid", out_kind="natural", name="proj_gates")

    groups = [_banded_group(*qkv_a[0], 0, 0)]
    groups += [_banded_group(*qkv_a[1], g, (g - 1) * HEADS_PER_GROUP_A)
               for g in range(1, len(DILATIONS))]
    yb = _dense_attention(qb, kb, vbt)
    merged = _gated_merge([o for o, _ in groups], [l for _, l in groups], yb,
                          p["w_a_br"], p["w_b_br"], gates)
    x1 = _oproj_residual(merged, p["w_o"], x)
    return _ffn_final(x1, p["g_ffn"], p["g_final"], p["w_gate_up"], p["w_down"])


def kernel(x_prompt, x_sample, g_attn, w_in, q_gain_b, k_gain_b, w_a_br, w_b_br, w_o, g_ffn,
           w_gate_up, w_down, g_final):
    assert g_attn.shape[0] == 1, "single-layer trunk"
    s = x_prompt.shape[1]
    assert x_sample.shape[1] == s and s % Q_CHUNK == 0
    p = {
        "g_attn": g_attn[0], "g_ffn": g_ffn[0], "g_final": g_final,
        "w_in": w_in[0].astype(BF16),
        "q_gain": q_gain_b[0].reshape(1, HEAD_DIM).astype(F32),
        "k_gain": k_gain_b[0].reshape(1, HEAD_DIM).astype(F32),
        "w_a_br": w_a_br[0].astype(BF16), "w_b_br": w_b_br[0].astype(BF16),
        "w_o": w_o[0].astype(BF16),
        "w_gate_up": w_gate_up[0].astype(BF16), "w_down": w_down[0].astype(BF16),
        "rope": _rope_tables(s), "axial": _axial_tables(s),
    }
    return (_trunk(x_prompt, p), _trunk(x_sample, p))
```

```python
import functools
import math

import numpy as np
import jax
import jax.numpy as jnp
from jax import lax
from jax.experimental import pallas as pl
from jax.experimental.pallas import tpu as pltpu

HEAD_DIM = 128
DILATIONS = (1, 4, 16)
BAND_RADIUS = 64
HEADS_PER_GROUP_A = 8
N_HEADS_A = HEADS_PER_GROUP_A * len(DILATIONS)
N_KV_B = 4
GQA_GROUP = 4
GRID_W = 64
ROPE_THETA = 10000.0
EPS = 1e-6
NEG_BIG = -1e30

VMEM_BYTES_V7X = 64 * 1024 * 1024
VMEM_LIMIT = VMEM_BYTES_V7X - 8 * 1024 * 1024
FFN_VMEM_LIMIT = VMEM_BYTES_V7X - 3 * 1024 * 1024

F32 = jnp.float32
BF16 = jnp.bfloat16

QSCALE = (HEAD_DIM ** -0.5) * math.log2(math.e)

Q_CHUNK = 4 * BAND_RADIUS * max(DILATIONS)
BAND_TILE = 128
SPLIT_STRIDE = 4

DENSE_TQ = 256
DENSE_TK = 1024
DENSE_UNROLL = 16
SUM_ROWS = 16


def _params(sem, vmem_limit=VMEM_LIMIT):
    return pltpu.CompilerParams(dimension_semantics=sem, vmem_limit_bytes=vmem_limit)


def _rmsnorm_kernel(x_ref, g_ref, o_ref):
    x = x_ref[...]
    ms = jnp.mean(x * x, axis=-1, keepdims=True)
    o_ref[...] = (x * lax.rsqrt(ms + EPS) * g_ref[...]).astype(o_ref.dtype)


def _rmsnorm_bf16(x, g, tm=512):
    b, s, d = x.shape
    return pl.pallas_call(
        _rmsnorm_kernel,
        grid=(b, s // tm),
        in_specs=[pl.BlockSpec((None, tm, d), lambda bi, i: (bi, i, 0)),
                  pl.BlockSpec((1, d), lambda bi, i: (0, 0))],
        out_specs=pl.BlockSpec((None, tm, d), lambda bi, i: (bi, i, 0)),
        out_shape=jax.ShapeDtypeStruct((b, s, d), BF16),
        compiler_params=_params(("parallel", "parallel")),
        name="rmsnorm_bf16",
    )(x, g.reshape(1, d).astype(F32))


def _proj_kernel(*refs, kind, out_kind, nh, rc, tm):
    h_ref, w_ref = refs[0], refs[1]
    o_ref = refs[-1]
    if kind in ("rope", "qknorm"):
        cos_ref, sin_ref = refs[2], refs[3]
    if kind == "qknorm":
        gain_ref = refs[4]
        lane = lax.broadcasted_iota(jnp.int32, (rc, HEAD_DIM), 1)
        lo = (lane % 64) < 32

    def epilogue(x, c, s):
        if kind == "rope":
            return x * c + pltpu.roll(x, 64, 1) * s
        if kind == "qknorm":
            ms = jnp.mean(x * x, axis=-1, keepdims=True)
            y = x * lax.rsqrt(ms + EPS) * gain_ref[...]
            yr = jnp.where(lo, pltpu.roll(y, 96, 1), pltpu.roll(y, 32, 1))
            return y * c + yr * s
        if kind == "sigmoid":
            return 1.0 / (1.0 + jnp.exp(-x))
        return x

    for r in range(tm // rc):
        r0, r1 = r * rc, (r + 1) * rc
        acc = jnp.dot(h_ref[r0:r1, :], w_ref[...], preferred_element_type=F32)
        c = s = None
        if kind in ("rope", "qknorm"):
            c, s = cos_ref[r0:r1, :], sin_ref[r0:r1, :]
        for j in range(nh):
            y = epilogue(acc[:, j * HEAD_DIM:(j + 1) * HEAD_DIM], c, s)
            if out_kind == "heads":
                o_ref[j, r0:r1, :] = y.astype(o_ref.dtype)
            elif out_kind == "natural":
                o_ref[r0:r1, j * HEAD_DIM:(j + 1) * HEAD_DIM] = y.astype(o_ref.dtype)
            else:
                o_ref[j, r] = y.T.astype(o_ref.dtype)


def _proj(h, w, col0, n, *, kind, out_kind, out_dtype=BF16, tables=(), gain=None,
          tm=2048, tn=1024, rc=256, name):
    b, s, d = h.shape
    tn = min(tn, n)
    assert col0 % tn == 0 and n % tn == 0 and s % tm == 0
    cb0 = col0 // tn
    nh = tn // HEAD_DIM

    in_specs = [pl.BlockSpec((None, tm, d), lambda bi, i, j: (bi, i, 0)),
                pl.BlockSpec((d, tn), lambda bi, i, j: (0, cb0 + j))]
    args = [h, w]
    for t in tables:
        in_specs.append(pl.BlockSpec((tm, HEAD_DIM), lambda bi, i, j: (i, 0)))
        args.append(t)
    if gain is not None:
        in_specs.append(pl.BlockSpec((1, HEAD_DIM), lambda bi, i, j: (0, 0)))
        args.append(gain)

    if out_kind == "heads":
        out_shape = (b, n // HEAD_DIM, s, HEAD_DIM)
        out_spec = pl.BlockSpec((None, nh, tm, HEAD_DIM), lambda bi, i, j: (bi, j, i, 0))
    elif out_kind == "natural":
        out_shape = (b, s, n)
        out_spec = pl.BlockSpec((None, tm, tn), lambda bi, i, j: (bi, i, j))
    else:
        out_shape = (b, n // HEAD_DIM, s // rc, HEAD_DIM, rc)
        out_spec = pl.BlockSpec((None, nh, tm // rc, HEAD_DIM, rc),
                                lambda bi, i, j: (bi, j, i, 0, 0))

    kern = functools.partial(_proj_kernel, kind=kind, out_kind=out_kind, nh=nh, rc=rc, tm=tm)
    return pl.pallas_call(
        kern,
        grid=(b, s // tm, n // tn),
        in_specs=in_specs,
        out_specs=out_spec,
        out_shape=jax.ShapeDtypeStruct(out_shape, out_dtype),
        compiler_params=_params(("parallel", "parallel", "arbitrary")),
        name=name,
    )(*args)


def _band_bias(t):
    qi = np.arange(t)[:, None]
    kj = np.arange(t + 2 * BAND_RADIUS)[None, :]
    ok = np.abs(kj - BAND_RADIUS - qi) <= BAND_RADIUS
    return np.where(ok, 0.0, NEG_BIG).astype(np.float32)


def _band_kernel(q_ref, kp_ref, kc_ref, kn_ref, vp_ref, vc_ref, vn_ref, bias_ref,
                 o_ref, lse_ref, *scratch, d, t, cm, seq_m):
    c = pl.program_id(2)
    rad = BAND_RADIUS
    win = t + 2 * rad
    operands = (q_ref, kp_ref, kc_ref, kn_ref, vp_ref, vc_ref, vn_ref)
    stages = scratch[2:]

    two_pass = d > SPLIT_STRIDE
    if two_pass:
        for ref, stage in zip(operands, stages):
            part = ref.shape[0] // SPLIT_STRIDE
            for r4 in range(SPLIT_STRIDE):
                stage[r4 * part:(r4 + 1) * part, :] = ref[pl.ds(r4, part, stride=SPLIT_STRIDE), :]

    def rows(i, r, n):
        ref = operands[i]
        if d == 1:
            return ref[...]
        if not two_pass:
            return ref[pl.ds(r, n, stride=d), :].astype(BF16)
        part = ref.shape[0] // SPLIT_STRIDE
        start = (r % SPLIT_STRIDE) * part + r // SPLIT_STRIDE
        return stages[i][pl.ds(start, n, stride=d // SPLIT_STRIDE), :].astype(BF16)

    biases = []
    for u in range(cm // t):
        kpos = c * cm + u * t - rad + lax.broadcasted_iota(jnp.int32, (1, win), 1)
        kvalid = jnp.where((kpos >= 0) & (kpos < seq_m), 0.0, NEG_BIG)
        biases.append(bias_ref[...] + kvalid)

    for r in range(d):
        q = rows(0, r, cm)
        kw = jnp.concatenate([rows(1, r, rad), rows(2, r, cm), rows(3, r, rad)], axis=0)
        vw = jnp.concatenate([rows(4, r, rad), rows(5, r, cm), rows(6, r, rad)], axis=0)
        for u in range(cm // t):
            m0 = u * t
            s = lax.dot_general(q[m0:m0 + t], kw[m0:m0 + win], (((1,), (1,)), ((), ())),
                                preferred_element_type=F32) + biases[u]
            m = jnp.max(s, axis=-1, keepdims=True)
            p = jnp.exp2(s - m)
            l = jnp.sum(p, axis=-1, keepdims=True)
            o = jnp.dot(p.astype(BF16), vw[m0:m0 + win], preferred_element_type=F32) / l
            lse = jnp.broadcast_to(m + jnp.log2(l), (t, HEAD_DIM))
            if d == 1:
                o_ref[m0:m0 + t, :] = o.astype(o_ref.dtype)
                lse_ref[m0:m0 + t, :] = lse
            else:
                o_scr, lse_scr = scratch[0], scratch[1]
                o_scr[pl.ds(r + d * m0, t, stride=d), :] = o
                lse_scr[pl.ds(r + d * m0, t, stride=d), :] = lse
    if d > 1:
        o_ref[...] = scratch[0][...].astype(o_ref.dtype)
        lse_ref[...] = scratch[1][...]


def _banded_group(q, k, v, g, head0):
    b, _, s, _ = q.shape
    d = DILATIONS[g]
    hpg = HEADS_PER_GROUP_A
    cm = Q_CHUNK // d
    t = min(BAND_TILE, cm)
    halo = BAND_RADIUS * d
    per = Q_CHUNK // halo
    last = s // halo - 1
    blk = pl.BlockSpec((None, None, Q_CHUNK, HEAD_DIM), lambda bi, hg, c: (bi, head0 + hg, c, 0))
    prev = pl.BlockSpec((None, None, halo, HEAD_DIM),
                        lambda bi, hg, c: (bi, head0 + hg, jnp.maximum(c * per - 1, 0), 0))
    nxt = pl.BlockSpec((None, None, halo, HEAD_DIM),
                       lambda bi, hg, c: (bi, head0 + hg, jnp.minimum((c + 1) * per, last), 0))
    out = pl.BlockSpec((None, None, Q_CHUNK, HEAD_DIM), lambda bi, hg, c: (bi, hg, c, 0))
    scratch = [] if d == 1 else [pltpu.VMEM((Q_CHUNK, HEAD_DIM), F32)] * 2
    if d > SPLIT_STRIDE:
        scratch += [pltpu.VMEM((n, HEAD_DIM), F32) for n in (Q_CHUNK,) + (halo, Q_CHUNK, halo) * 2]
    return pl.pallas_call(
        functools.partial(_band_kernel, d=d, t=t, cm=cm, seq_m=s // d),
        grid=(b, hpg, s // Q_CHUNK),
        in_specs=[blk, prev, blk, nxt, prev, blk, nxt,
                  pl.BlockSpec((t, t + 2 * BAND_RADIUS), lambda bi, hg, c: (0, 0))],
        out_specs=[out, out],
        out_shape=[jax.ShapeDtypeStruct((b, hpg, s, HEAD_DIM), BF16),
                   jax.ShapeDtypeStruct((b, hpg, s, HEAD_DIM), F32)],
        scratch_shapes=scratch,
        compiler_params=_params(("parallel", "parallel", "arbitrary")),
        name=f"banded_attention_d{d}",
    )(q, k, k, k, v, v, v, jnp.asarray(_band_bias(t)))


def _dense_kernel(q_ref, k_ref, vt_ref, o_ref, acc_ref, *, tq, tk):
    n_chunks = k_ref.shape[0] // tk
    heads = range(GQA_GROUP)

    def strip(g):
        return slice(g * tq, (g + 1) * tq)

    acc_ref[...] = jnp.zeros(acc_ref.shape, F32)

    ones_rows = jnp.ones((SUM_ROWS, tk), BF16)

    def step(c, g, m):
        k0 = pl.multiple_of(c * tk, tk)
        q = q_ref[:, g * HEAD_DIM:(g + 1) * HEAD_DIM]
        st = lax.dot_general(k_ref[pl.ds(k0, tk), :], q, (((1,), (1,)), ((), ())),
                             preferred_element_type=F32)
        m_new = jnp.maximum(m, jnp.max(st, axis=0, keepdims=True))
        alpha = jnp.exp2(m - m_new)
        p = jnp.exp2(st - m_new).astype(BF16)
        vt1 = jnp.concatenate([vt_ref[c], ones_rows], axis=0)
        pv = jnp.dot(vt1, p, preferred_element_type=F32)
        acc_ref[:, strip(g)] = acc_ref[:, strip(g)] * alpha + pv
        return m_new

    unroll = math.gcd(DENSE_UNROLL, n_chunks)

    def trip(j, ms):
        for u in range(unroll):
            ms = tuple(step(unroll * j + u, g, ms[g]) for g in heads)
        return ms

    m0 = tuple(jnp.full((1, tq), NEG_BIG, F32) for _ in heads)
    lax.fori_loop(0, n_chunks // unroll, trip, m0)
    for g in heads:
        o = acc_ref[0:HEAD_DIM, strip(g)] / acc_ref[HEAD_DIM:HEAD_DIM + 1, strip(g)]
        o_ref[:, g * HEAD_DIM:(g + 1) * HEAD_DIM] = o.T.astype(o_ref.dtype)


def _dense_attention(qb, kb, vbt, tq=DENSE_TQ):
    b, s, dq = qb.shape
    tk = vbt.shape[-1]
    gw = GQA_GROUP * HEAD_DIM
    n_tiles = s // tq
    return pl.pallas_call(
        functools.partial(_dense_kernel, tq=tq, tk=tk),
        grid=(b, N_KV_B, n_tiles),
        in_specs=[pl.BlockSpec((None, tq, gw), lambda bi, h, i: (bi, i, h)),
                  pl.BlockSpec((None, None, s, HEAD_DIM), lambda bi, h, i: (bi, h, 0, 0)),
                  pl.BlockSpec((None, None, s // tk, HEAD_DIM, tk),
                               lambda bi, h, i: (bi, h, 0, 0, 0))],
        out_specs=pl.BlockSpec((None, tq, gw), lambda bi, h, i: (bi, i, h)),
        out_shape=jax.ShapeDtypeStruct((b, s, dq), BF16),
        scratch_shapes=[pltpu.VMEM((HEAD_DIM + SUM_ROWS, GQA_GROUP * tq), F32)],
        compiler_params=_params(("parallel", "parallel", "arbitrary")),
        name="dense_gqa_attention",
    )(qb, kb, vbt)


def _merge_kernel(*refs, tm, rc, ng):
    o_refs, lse_refs = refs[0:ng], refs[ng:2 * ng]
    yb_ref, wa_ref, wb_ref, g_ref, out_ref, ya_ref = refs[2 * ng:]
    d = out_ref.shape[-1]

    for h in range(HEADS_PER_GROUP_A):
        lses = [lse_refs[g][h] for g in range(ng)]
        top = functools.reduce(jnp.maximum, lses)
        es = [jnp.exp2(x - top) for x in lses]
        num = sum(e * o_refs[g][h].astype(F32) for g, e in enumerate(es))
        ya_ref[:, h * HEAD_DIM:(h + 1) * HEAD_DIM] = (num / sum(es)).astype(ya_ref.dtype)

    for r in range(tm // rc):
        r0, r1 = r * rc, (r + 1) * rc
        pa = jnp.dot(ya_ref[r0:r1, :], wa_ref[...], preferred_element_type=F32)
        pb = jnp.dot(yb_ref[r0:r1, :], wb_ref[...], preferred_element_type=F32)
        sa = g_ref[r0:r1, 0:d].astype(F32)
        sb = g_ref[r0:r1, d:2 * d].astype(F32)
        out_ref[r0:r1, :] = (sa * pa + sb * pb).astype(out_ref.dtype)


def _gated_merge(group_outs, group_lses, yb, wa, wb, gates, tm=512, rc=256):
    ng = len(group_outs)
    b, hpg, s, _ = group_outs[0].shape
    da = hpg * HEAD_DIM
    db = yb.shape[-1]
    d = wa.shape[1]
    hblk = pl.BlockSpec((None, hpg, tm, HEAD_DIM), lambda bi, i: (bi, 0, i, 0))
    once = pl.Buffered(1)
    return pl.pallas_call(
        functools.partial(_merge_kernel, tm=tm, rc=rc, ng=ng),
        grid=(b, s // tm),
        in_specs=[hblk] * (2 * ng) + [
            pl.BlockSpec((None, tm, db), lambda bi, i: (bi, i, 0)),
            pl.BlockSpec((da, d), lambda bi, i: (0, 0), pipeline_mode=once),
            pl.BlockSpec((db, d), lambda bi, i: (0, 0), pipeline_mode=once),
            pl.BlockSpec((None, tm, 2 * d), lambda bi, i: (bi, i, 0))],
        out_specs=pl.BlockSpec((None, tm, d), lambda bi, i: (bi, i, 0)),
        out_shape=jax.ShapeDtypeStruct((b, s, d), BF16),
        scratch_shapes=[pltpu.VMEM((tm, da), BF16)],
        compiler_params=_params(("parallel", "parallel")),
        name="gated_merge",
    )(*group_outs, *group_lses, yb, wa, wb, gates)


def _oproj_kernel(m_ref, w_ref, x_ref, o_ref, *, tm, rc):
    for r in range(tm // rc):
        r0, r1 = r * rc, (r + 1) * rc
        o_ref[r0:r1, :] = x_ref[r0:r1, :] + jnp.dot(m_ref[r0:r1, :], w_ref[...],
                                                    preferred_element_type=F32)


def _oproj_residual(merged, w_o, x, tm=1024, rc=256):
    b, s, d = merged.shape
    return pl.pallas_call(
        functools.partial(_oproj_kernel, tm=tm, rc=rc),
        grid=(b, s // tm),
        in_specs=[pl.BlockSpec((None, tm, d), lambda bi, i: (bi, i, 0)),
                  pl.BlockSpec((d, d), lambda bi, i: (0, 0), pipeline_mode=pl.Buffered(1)),
                  pl.BlockSpec((None, tm, d), lambda bi, i: (bi, i, 0))],
        out_specs=pl.BlockSpec((None, tm, d), lambda bi, i: (bi, i, 0)),
        out_shape=jax.ShapeDtypeStruct((b, s, d), F32),
        compiler_params=_params(("parallel", "parallel")),
        name="oproj_residual",
    )(merged, w_o, x)


def _ffn_kernel(x_ref, gffn_ref, gfin_ref, wg_ref, wu_ref, wd_ref, o_ref, h_ref, *, tm, rc):
    f = pl.program_id(2)

    @pl.when(f == 0)
    def _():
        x = x_ref[...]
        ms = jnp.mean(x * x, axis=-1, keepdims=True)
        h_ref[...] = (x * lax.rsqrt(ms + EPS) * gffn_ref[...]).astype(h_ref.dtype)
        o_ref[...] = jnp.zeros(o_ref.shape, F32)

    for r in range(tm // rc):
        r0, r1 = r * rc, (r + 1) * rc
        h = h_ref[r0:r1, :]
        gate = jnp.dot(h, wg_ref[...], preferred_element_type=F32)
        up = jnp.dot(h, wu_ref[...], preferred_element_type=F32)
        act = (gate / (1.0 + jnp.exp(-gate))) * up
        o_ref[r0:r1, :] += jnp.dot(act.astype(BF16), wd_ref[...], preferred_element_type=F32)

    @pl.when(f == pl.num_programs(2) - 1)
    def _():
        y = x_ref[...] + o_ref[...]
        ms = jnp.mean(y * y, axis=-1, keepdims=True)
        o_ref[...] = y * lax.rsqrt(ms + EPS) * gfin_ref[...]


def _ffn_final(x, g_ffn, g_final, w_gu, w_down, tm=1024, tf=512, rc=512):
    b, s, d = x.shape
    dff = w_down.shape[0]
    nf = dff // tf
    return pl.pallas_call(
        functools.partial(_ffn_kernel, tm=tm, rc=rc),
        grid=(b, s // tm, nf),
        in_specs=[pl.BlockSpec((None, tm, d), lambda bi, i, f: (bi, i, 0)),
                  pl.BlockSpec((1, d), lambda bi, i, f: (0, 0)),
                  pl.BlockSpec((1, d), lambda bi, i, f: (0, 0)),
                  pl.BlockSpec((d, tf), lambda bi, i, f: (0, f)),
                  pl.BlockSpec((d, tf), lambda bi, i, f: (0, nf + f)),
                  pl.BlockSpec((tf, d), lambda bi, i, f: (f, 0))],
        out_specs=pl.BlockSpec((None, tm, d), lambda bi, i, f: (bi, i, 0)),
        out_shape=jax.ShapeDtypeStruct((b, s, d), F32),
        scratch_shapes=[pltpu.VMEM((tm, d), BF16)],
        compiler_params=_params(("parallel", "parallel", "arbitrary"), vmem_limit=FFN_VMEM_LIMIT),
        name="ffn_final",
    )(x, g_ffn.reshape(1, d).astype(F32), g_final.reshape(1, d).astype(F32), w_gu, w_gu, w_down)


def _rope_tables(s):
    half = HEAD_DIM // 2
    inv = ROPE_THETA ** (-(jnp.arange(half, dtype=F32) * 2.0 / HEAD_DIM))
    ang = jnp.arange(s).astype(F32)[:, None] * inv[None, :]
    cos, sin = jnp.cos(ang), jnp.sin(ang)
    return jnp.concatenate([cos, cos], axis=-1), jnp.concatenate([-sin, sin], axis=-1)


def _axial_tables(s):
    dim = HEAD_DIM // 2
    half = dim // 2
    inv = ROPE_THETA ** (-(jnp.arange(half, dtype=F32) * 2.0 / dim))
    n_rows = s // GRID_W
    row = jnp.repeat(jnp.arange(n_rows), GRID_W).astype(F32)
    col = jnp.tile(jnp.arange(GRID_W), n_rows).astype(F32)
    cs, sn = [], []
    for pos in (row, col):
        ang = pos[:, None] * inv[None, :]
        c, sgn = jnp.cos(ang), jnp.sin(ang)
        cs += [c, c]
        sn += [-sgn, sgn]
    return jnp.concatenate(cs, axis=-1), jnp.concatenate(sn, axis=-1)


def _trunk(x, p):
    b, s, d = x.shape
    a_w = N_HEADS_A * HEAD_DIM
    b_q = d
    b_kv = N_KV_B * HEAD_DIM
    cuts = np.cumsum([0, a_w, a_w, a_w, b_q, b_kv, b_kv, 2 * d]).tolist()
    w_in = p["w_in"]

    def wcols(k):
        return w_in, cuts[k], cuts[k + 1] - cuts[k]

    h = _rmsnorm_bf16(x, p["g_attn"])
    cos1, sin1 = p["rope"]
    cos2, sin2 = p["axial"]

    gw = HEADS_PER_GROUP_A * HEAD_DIM
    qkv_a = []
    for col0, n, dt, tag in ((0, gw, BF16, "g0"), (gw, a_w - gw, F32, "g12")):
        qkv_a.append((
            _proj(h, w_in, cuts[0] + col0, n, kind="rope", out_kind="heads", out_dtype=dt,
                  tables=(cos1 * QSCALE, sin1 * QSCALE), name="proj_qa_" + tag),
            _proj(h, w_in, cuts[1] + col0, n, kind="rope", out_kind="heads", out_dtype=dt,
                  tables=(cos1, sin1), name="proj_ka_" + tag),
            _proj(h, w_in, cuts[2] + col0, n, kind="plain", out_kind="heads", out_dtype=dt,
                  name="proj_va_" + tag)))
    qb = _proj(h, *wcols(3), kind="qknorm", out_kind="natural", tables=(cos2, sin2),
               gain=p["q_gain"] * QSCALE, name="proj_qb")
    kb = _proj(h, *wcols(4), kind="qknorm", out_kind="heads", tables=(cos2, sin2),
               gain=p["k_gain"], name="proj_kb")
    vbt = _proj(h, *wcols(5), kind="plain", out_kind="vT", rc=DENSE_TK, name="proj_vb")
    gates = _proj(h, *wcols(6), kind="sigmoid", out_kind="natural", name="proj_gates")

    groups = [_banded_group(*qkv_a[0], 0, 0)]
    groups += [_banded_group(*qkv_a[1], g, (g - 1) * HEADS_PER_GROUP_A)
               for g in range(1, len(DILATIONS))]
    yb = _dense_attention(qb, kb, vbt)
    merged = _gated_merge([o for o, _ in groups], [l for _, l in groups], yb,
                          p["w_a_br"], p["w_b_br"], gates)
    x1 = _oproj_residual(merged, p["w_o"], x)
    return _ffn_final(x1, p["g_ffn"], p["g_final"], p["w_gate_up"], p["w_down"])


def kernel(x_prompt, x_sample, g_attn, w_in, q_gain_b, k_gain_b, w_a_br, w_b_br, w_o, g_ffn,
           w_gate_up, w_down, g_final):
    assert g_attn.shape[0] == 1, "single-layer trunk"
    s = x_prompt.shape[1]
    assert x_sample.shape[1] == s and s % Q_CHUNK == 0
    p = {
        "g_attn": g_attn[0], "g_ffn": g_ffn[0], "g_final": g_final,
        "w_in": w_in[0].astype(BF16),
        "q_gain": q_gain_b[0].reshape(1, HEAD_DIM).astype(F32),
        "k_gain": k_gain_b[0].reshape(1, HEAD_DIM).astype(F32),
        "w_a_br": w_a_br[0].astype(BF16), "w_b_br": w_b_br[0].astype(BF16),
        "w_o": w_o[0].astype(BF16),
        "w_gate_up": w_gate_up[0].astype(BF16), "w_down": w_down[0].astype(BF16),
        "rope": _rope_tables(s), "axial": _axial_tables(s),
    }
    return (_trunk(x_prompt, p), _trunk(x_sample, p))
```
